```python
import functools
import numpy as np
import jax
import jax.numpy as jnp
from jax import lax

D_MODEL = 1024
BATCH = 2
SEQ = 8192
DEPTH = 1
DEC_BATCH = 128
DEC_SEQ = 4
PAST_LEN = 8192
PAGE_SIZE = 128

D_CONV = D_MODEL
CONV_W = 3
N_HEADS = 16
HEAD_DIM = 64
N_KV = 4
GROUP = N_HEADS // N_KV
D_ATT = N_HEADS * HEAD_DIM
KV_W = N_KV * HEAD_DIM
CMP_BLK = 32
CMP_STRIDE = 16
CMP_RATIO = CMP_BLK // CMP_STRIDE
CMP_HID = 128
SLC_BLK = 64
TOPK = 16
WINDOW = 512
Q_BLK = 128
D_FF = 2816
EPS = 1e-6
NEG = -1e30
BIG = 1e30
SCALE = HEAD_DIM ** -0.5
IN_SPLITS = [D_CONV, D_CONV, D_CONV, D_ATT] + [KV_W] * 6 + [3 * N_HEADS, 2 * D_MODEL]
IN_W = sum(IN_SPLITS)

kernel_name = 'hybrid_conv_nsa_convffn_decode_step'


def rmsnorm(x, g):
    x32 = x.astype(jnp.float32)
    y = x32 * lax.rsqrt(jnp.mean(x32 * x32, axis=-1, keepdims=True) + EPS)
    return y.astype(x.dtype) * g


def modulate(x, g, shift, scale):
    return rmsnorm(x, g) * (1 + scale[:, None, :]) + shift[:, None, :]


def causal_dwconv(u, prev, w):
    T = u.shape[1]
    ext = jnp.concatenate([prev, u], axis=1)
    y = sum(w[k] * ext[:, k:k + T] for k in range(CONV_W))
    return y, ext[:, T:]


def masked_softmax(s, mask):
    s = jnp.where(mask, s, NEG)
    m = jnp.max(s, axis=-1, keepdims=True)
    e = jnp.where(mask, jnp.exp(s - m), 0.0)
    return e / jnp.maximum(jnp.sum(e, axis=-1, keepdims=True), 1e-30)


def compress(k, pos_emb, w1, b1, w2):
    b, L = k.shape[:2]
    n_chunk = L // CMP_STRIDE
    n_cmp = n_chunk - CMP_RATIO + 1
    kk = k[:, :n_chunk * CMP_STRIDE].reshape(b, n_chunk, CMP_STRIDE, N_KV, HEAD_DIM)
    w1r = w1.reshape(CMP_RATIO, CMP_STRIDE, HEAD_DIM, CMP_HID)
    part = jnp.einsum('bnsgd,rsdh->rbngh', kk, w1r)
    hid = sum(part[r][:, r:r + n_cmp] for r in range(CMP_RATIO))
    hid = hid + (pos_emb.reshape(-1) @ w1 + b1)
    return jax.nn.silu(hid) @ w2


def overlap_matrix(n_cmp, n_slc):
    i = np.arange(n_cmp)[:, None] * CMP_STRIDE
    j = np.arange(n_slc)[None, :] * SLC_BLK
    ovl = (i <= j + SLC_BLK - 1) & (i + CMP_BLK - 1 >= j)
    return jnp.asarray(ovl.astype(np.float32))


def cmp_branch(q, qpos, kc, vc):
    s = jnp.einsum('bqgrd,bngd->bqgrn', q, kc).astype(jnp.float32) * SCALE
    end = jnp.arange(kc.shape[1], dtype=jnp.int32) * CMP_STRIDE + CMP_BLK - 1
    mask = (end[None, :] <= qpos[:, None])[None, :, None, None, :]
    p = masked_softmax(s, mask)
    o = jnp.einsum('bqgrn,bngd->bqgrd', p.astype(vc.dtype), vc)
    return o, p


def select_blocks(p, qpos, n_slc):
    imp = jnp.einsum('bqgrn,ns->bqgs', p, overlap_matrix(p.shape[-1], n_slc))
    j = jnp.arange(n_slc, dtype=jnp.int32)[None, :]
    tb = (qpos // SLC_BLK)[:, None]
    valid = j * SLC_BLK <= qpos[:, None]
    forced = (j == 0) | (j == tb) | (j == tb - 1)
    score = jnp.where(forced[None, :, None], BIG, jnp.where(valid[None, :, None], imp, NEG))
    _, idx = lax.top_k(score, min(TOPK, n_slc))
    return idx


def sel_branch(q, qpos, ks, vs, idx):
    b, nq, g, kk = idx.shape
    ks = ks.reshape(b, nq, g, kk * SLC_BLK, HEAD_DIM)
    vs = vs.reshape(b, nq, g, kk * SLC_BLK, HEAD_DIM)
    kpos = (idx[..., None] * SLC_BLK + jnp.arange(SLC_BLK, dtype=jnp.int32)).reshape(b, nq, g, kk * SLC_BLK)
    s = jnp.einsum('bqgrd,bqgnd->bqgrn', q, ks).astype(jnp.float32) * SCALE
    mask = (kpos <= qpos[None, :, None, None])[:, :, :, None, :]
    p = masked_softmax(s, mask)
    return jnp.einsum('bqgrn,bqgnd->bqgrd', p.astype(vs.dtype), vs)


def win_branch(q, qpos, kw, vw, kpos):
    s = jnp.einsum('bqgrd,bngd->bqgrn', q, kw).astype(jnp.float32) * SCALE
    kp, qp = kpos[None, :], qpos[:, None]
    mask = ((kp <= qp) & (kp > qp - WINDOW) & (kp >= 0))[None, :, None, None, :]
    p = masked_softmax(s, mask)
    return jnp.einsum('bqgrn,bngd->bqgrd', p.astype(vw.dtype), vw)


def nsa_combine(gates, oc, os_, ow):
    return gates[:, :, 0][..., None] * oc + gates[:, :, 1][..., None] * os_ + gates[:, :, 2][..., None] * ow


def nsa_prompt(q, gates, kvs, ck, cv):
    k_c, v_c, k_s, v_s, k_w, v_w = kvs
    b, T = q.shape[:2]
    kc = compress(k_c, *ck)
    vc = compress(v_c, *cv)
    n_slc = T // SLC_BLK
    ks_src = k_s.reshape(b * n_slc, SLC_BLK, N_KV, HEAD_DIM)
    vs_src = v_s.reshape(b * n_slc, SLC_BLK, N_KV, HEAD_DIM)
    pad = ((0, 0), (WINDOW, 0), (0, 0), (0, 0))
    kw_pad = jnp.pad(k_w, pad)
    vw_pad = jnp.pad(v_w, pad)
    bi = jnp.arange(b, dtype=jnp.int32)[:, None, None, None]
    gi = jnp.arange(N_KV, dtype=jnp.int32)[None, None, :, None]

    def one_block(i):
        s0 = i * Q_BLK
        qb = lax.dynamic_slice_in_dim(q, s0, Q_BLK, axis=1)
        gb = lax.dynamic_slice_in_dim(gates, s0, Q_BLK, axis=1)
        qpos = s0 + jnp.arange(Q_BLK, dtype=jnp.int32)
        oc, p = cmp_branch(qb, qpos, kc, vc)
        idx = select_blocks(p, qpos, n_slc)
        pidx = bi * n_slc + idx
        os_ = sel_branch(qb, qpos, ks_src[pidx, :, gi], vs_src[pidx, :, gi], idx)
        kwb = lax.dynamic_slice_in_dim(kw_pad, s0, WINDOW + Q_BLK, axis=1)
        vwb = lax.dynamic_slice_in_dim(vw_pad, s0, WINDOW + Q_BLK, axis=1)
        kpos = s0 - WINDOW + jnp.arange(WINDOW + Q_BLK, dtype=jnp.int32)
        ow = win_branch(qb, qpos, kwb, vwb, kpos)
        return nsa_combine(gb, oc, os_, ow)

    out = lax.map(one_block, jnp.arange(T // Q_BLK, dtype=jnp.int32))
    y = jnp.moveaxis(out, 0, 1).reshape(b, T, D_ATT)
    wb = min(WINDOW, T)
    return y, (k_c, v_c, k_s, v_s, k_w[:, T - wb:], v_w[:, T - wb:])


def nsa_sample(q, gates, kvs, ck, cv, caches, page_table):
    k_c, v_c, k_s, v_s, k_w, v_w = kvs
    pool_ck, pool_cv, pool_sk, pool_sv, win_k, win_v = caches
    b, S = q.shape[:2]
    P = page_table.shape[1] * PAGE_SIZE
    qpos = P + jnp.arange(S, dtype=jnp.int32)

    def past(pool):
        return pool[page_table].reshape(b, P, N_KV, HEAD_DIM)

    kc = compress(jnp.concatenate([past(pool_ck), k_c], axis=1), *ck)
    vc = compress(jnp.concatenate([past(pool_cv), v_c], axis=1), *cv)
    oc, p = cmp_branch(q, qpos, kc, vc)
    n_slc = -(-(P + S) // SLC_BLK)
    idx = select_blocks(p, qpos, n_slc)
    bpp = PAGE_SIZE // SLC_BLK
    pb = P // SLC_BLK
    n_new = -(-S // SLC_BLK)
    bi = jnp.arange(b, dtype=jnp.int32)[:, None, None, None]
    gi = jnp.arange(N_KV, dtype=jnp.int32)[None, None, :, None]
    jp = jnp.minimum(idx, pb - 1)
    pidx = page_table[bi, jp // bpp] * bpp + jp % bpp
    nidx = bi * n_new + jnp.clip(idx - pb, 0, n_new - 1)
    in_past = (idx < pb)[..., None, None]

    def gather_sel(pool, new):
        src_old = pool.reshape(-1, SLC_BLK, N_KV, HEAD_DIM)
        src_new = jnp.pad(new, ((0, 0), (0, n_new * SLC_BLK - S), (0, 0), (0, 0)))
        src_new = src_new.reshape(b * n_new, SLC_BLK, N_KV, HEAD_DIM)
        return jnp.where(in_past, src_old[pidx, :, gi], src_new[nidx, :, gi])

    os_ = sel_branch(q, qpos, gather_sel(pool_sk, k_s), gather_sel(pool_sv, v_s), idx)
    wb = win_k.shape[1]
    kw = jnp.concatenate([win_k, k_w], axis=1)
    vw = jnp.concatenate([win_v, v_w], axis=1)
    kpos = P - wb + jnp.arange(wb + S, dtype=jnp.int32)
    ow = win_branch(q, qpos, kw, vw, kpos)
    y = nsa_combine(gates, oc, os_, ow).reshape(b, S, D_ATT)
    return y, (k_c, v_c, k_s, v_s, kw[:, S:], vw[:, S:])


def trunk_layer(x, c, lw, ck, cv, conv_prev, ffn_prev, nsa_fn):
    w_ada, b_ada, norm1_g, norm2_g, w_in, conv_w, w_out, w_up, ffn_conv_w, w_down = lw
    b, T, _ = x.shape
    sh1, sc1, gt1, sh2, sc2, gt2 = jnp.split(c @ w_ada + b_ada, 6, axis=-1)
    h = modulate(x, norm1_g, sh1, sc1)
    z = h @ w_in
    cuts = [int(v) for v in np.cumsum(IN_SPLITS[:-1])]
    bg, cg, xin, q, kc, vc, ks, vs, kw, vw, ag, mg = jnp.split(z, cuts, axis=-1)
    conv_out, conv_new = causal_dwconv(cg * xin, conv_prev, conv_w)
    y_a = bg * conv_out
    kvs = tuple(t.reshape(b, T, N_KV, HEAD_DIM) for t in (kc, vc, ks, vs, kw, vw))
    gates = jax.nn.sigmoid(ag).reshape(b, T, 3, N_KV, GROUP)
    y_b, att_state = nsa_fn(q.reshape(b, T, N_KV, GROUP, HEAD_DIM), gates, kvs, ck, cv)
    g_a, g_b = jnp.split(jax.nn.sigmoid(mg), 2, axis=-1)
    x = x + gt1[:, None, :] * ((g_a * y_a + g_b * y_b) @ w_out)
    h2 = modulate(x, norm2_g, sh2, sc2)
    u, v = jnp.split(h2 @ w_up, 2, axis=-1)
    u_c, ffn_new = causal_dwconv(u, ffn_prev, ffn_conv_w)
    x = x + gt2[:, None, :] * ((jax.nn.silu(u_c) * v) @ w_down)
    return x, att_state, conv_new, ffn_new


def setup_inputs(seed: int = 0) -> dict:
    key = jax.random.key(seed)
    keys = iter(jax.random.split(key, 40))

    def nrm(shape, scale):
        return jax.random.normal(next(keys), shape, jnp.float32) * scale

    n_pages = PAST_LEN // PAGE_SIZE
    n_used = DEC_BATCH * n_pages
    n_phys = n_used + n_used // 4
    wb = min(WINDOW, PAST_LEN)
    pool = (DEPTH, n_phys, PAGE_SIZE, N_KV, HEAD_DIM)
    win = (DEPTH, DEC_BATCH, wb, N_KV, HEAD_DIM)
    page_table = jax.random.permutation(next(keys), n_phys)[:n_used].reshape(DEC_BATCH, n_pages).astype(jnp.int32)
    fin_c = CMP_BLK * HEAD_DIM
    return {
        'x_prompt': nrm((BATCH, SEQ, D_MODEL), 1.0),
        'x_sample': nrm((DEC_BATCH, DEC_SEQ, D_MODEL), 1.0),
        'cache_cmp_k': nrm(pool, 1.0),
        'cache_cmp_v': nrm(pool, 1.0),
        'cache_slc_k': nrm(pool, 1.0),
        'cache_slc_v': nrm(pool, 1.0),
        'cache_win_k': nrm(win, 1.0),
        'cache_win_v': nrm(win, 1.0),
        'state_conv': nrm((DEPTH, DEC_BATCH, CONV_W - 1, D_CONV), 1.0),
        'state_ffn': nrm((DEPTH, DEC_BATCH, CONV_W - 1, D_FF), 1.0),
        'page_table': page_table,
        'c_prompt': nrm((BATCH, D_MODEL), 1.0),
        'c_sample': nrm((DEC_BATCH, D_MODEL), 1.0),
        'w_ada': nrm((DEPTH, D_MODEL, 6 * D_MODEL), 0.5 * D_MODEL ** -0.5),
        'b_ada': nrm((DEPTH, 6 * D_MODEL), 0.01),
        'norm1_g': 1.0 + nrm((DEPTH, D_MODEL), 0.02),
        'norm2_g': 1.0 + nrm((DEPTH, D_MODEL), 0.02),
        'w_in': nrm((DEPTH, D_MODEL, IN_W), D_MODEL ** -0.5),
        'conv_w': nrm((DEPTH, CONV_W, D_CONV), CONV_W ** -0.5),
        'cmp_pos_k': nrm((DEPTH, CMP_BLK, HEAD_DIM), 0.02),
        'cmp_w1_k': nrm((DEPTH, fin_c, CMP_HID), fin_c ** -0.5),
        'cmp_b1_k': nrm((DEPTH, CMP_HID), 0.01),
        'cmp_w2_k': nrm((DEPTH, CMP_HID, HEAD_DIM), CMP_HID ** -0.5),
        'cmp_pos_v': nrm((DEPTH, CMP_BLK, HEAD_DIM), 0.02),
        'cmp_w1_v': nrm((DEPTH, fin_c, CMP_HID), fin_c ** -0.5),
        'cmp_b1_v': nrm((DEPTH, CMP_HID), 0.01),
        'cmp_w2_v': nrm((DEPTH, CMP_HID, HEAD_DIM), CMP_HID ** -0.5),
        'w_out': nrm((DEPTH, D_MODEL, D_MODEL), D_MODEL ** -0.5),
        'w_up': nrm((DEPTH, D_MODEL, 2 * D_FF), D_MODEL ** -0.5),
        'ffn_conv_w': nrm((DEPTH, CONV_W, D_FF), CONV_W ** -0.5),
        'w_down': nrm((DEPTH, D_FF, D_MODEL), D_FF ** -0.5),
        'norm_f_g': 1.0 + nrm((D_MODEL,), 0.02),
    }


def reference(x_prompt, x_sample, cache_cmp_k, cache_cmp_v, cache_slc_k, cache_slc_v, cache_win_k, cache_win_v,
              state_conv, state_ffn, page_table, c_prompt, c_sample, w_ada, b_ada, norm1_g, norm2_g, w_in, conv_w,
              cmp_pos_k, cmp_w1_k, cmp_b1_k, cmp_w2_k, cmp_pos_v, cmp_w1_v, cmp_b1_v, cmp_w2_v,
              w_out, w_up, ffn_conv_w, w_down, norm_f_g):
    xp, xs = x_prompt, x_sample
    st_p, st_s = [], []
    for l in range(DEPTH):
        lw = (w_ada[l], b_ada[l], norm1_g[l], norm2_g[l], w_in[l], conv_w[l], w_out[l], w_up[l], ffn_conv_w[l], w_down[l])
        ck = (cmp_pos_k[l], cmp_w1_k[l], cmp_b1_k[l], cmp_w2_k[l])
        cv = (cmp_pos_v[l], cmp_w1_v[l], cmp_b1_v[l], cmp_w2_v[l])
        conv0 = jnp.zeros((xp.shape[0], CONV_W - 1, D_CONV), xp.dtype)
        ffn0 = jnp.zeros((xp.shape[0], CONV_W - 1, D_FF), xp.dtype)
        xp, att_p, cp, fp = trunk_layer(xp, c_prompt, lw, ck, cv, conv0, ffn0, nsa_prompt)
        caches = (cache_cmp_k[l], cache_cmp_v[l], cache_slc_k[l], cache_slc_v[l], cache_win_k[l], cache_win_v[l])
        nsa_s = functools.partial(nsa_sample, caches=caches, page_table=page_table)
        xs, att_s, cs, fs = trunk_layer(xs, c_sample, lw, ck, cv, state_conv[l], state_ffn[l], nsa_s)
        st_p.append((*att_p, cp, fp))
        st_s.append((*att_s, cs, fs))
    y_prompt = rmsnorm(xp, norm_f_g)
    y_sample = rmsnorm(xs, norm_f_g)
    cmp_k_p, cmp_v_p, slc_k_p, slc_v_p, win_k_p, win_v_p, conv_p, ffn_p = [jnp.stack(a) for a in zip(*st_p)]
    cmp_k_s, cmp_v_s, slc_k_s, slc_v_s, win_k_s, win_v_s, conv_s, ffn_s = [jnp.stack(a) for a in zip(*st_s)]
    return (y_prompt, y_sample, cmp_k_p, cmp_v_p, slc_k_p, slc_v_p, win_k_p, win_v_p, conv_p, ffn_p,
            cmp_k_s, cmp_v_s, slc_k_s, slc_v_s, win_k_s, win_v_s, conv_s, ffn_s)
```

```python
import functools

import numpy as np
import jax
import jax.numpy as jnp
from jax import lax
from jax.experimental import pallas as pl
from jax.experimental.pallas import tpu as pltpu

F32 = jnp.float32
BF16 = jnp.bfloat16

D_MODEL = 1024
N_KV = 4
GROUP = 4
HEAD_DIM = 64
KV_W = N_KV * HEAD_DIM
CMP_BLK = 32
CMP_STRIDE = 16
CMP_HID = 128
SLC_BLK = 64
TOPK = 16
WINDOW = 512
PAGE = 128
D_FF = 2816
EPS = 1e-6
NEG = -1e30
BIG = 1e30
SCALE = HEAD_DIM ** -0.5

LANES = 128
SUBLANES = 8
VMEM_LIMIT = 56 * 1024 * 1024

COL_BG, COL_CG, COL_XIN, COL_Q, COL_MG, COL_KV = 0, 1024, 2048, 3072, 4096, 6144
Z_W = 7680
AG_W = N_KV * LANES
IN_OFF_Q = 3072
IN_OFF_KV = 4096
IN_OFF_AG = 5632
IN_OFF_MG = 5680

KEY_TILE = 512
WIN_KEYS = WINDOW + 128
Q_BLK = 128


def _cparams(sem, vmem=VMEM_LIMIT):
    return pltpu.CompilerParams(dimension_semantics=sem, vmem_limit_bytes=vmem)


def _dot(a, b):
    return jnp.dot(a, b, preferred_element_type=F32)


def _dot_nt(a, b):
    return lax.dot_general(a, b, (((1,), (1,)), ((), ())), preferred_element_type=F32)


def _sigmoid(x):
    return 1.0 / (1.0 + jnp.exp(-x))


def _softmax_parts(s, mask):
    s = jnp.where(mask, s, NEG)
    m = jnp.max(s, axis=-1, keepdims=True)
    return s, m


def _topk_select(score, lane, k, want_idx):
    rows, n = score.shape
    sel = jnp.zeros((rows, n), F32)
    idx_mat = jnp.zeros((rows, LANES), F32) if want_idx else None
    out_lane = lax.broadcasted_iota(jnp.int32, (rows, LANES), 1) if want_idx else None
    for i in range(k):
        m = jnp.max(score, axis=-1, keepdims=True)
        cand = jnp.where(score == m, lane, float(n))
        idx = jnp.min(cand, axis=-1, keepdims=True)
        hit = lane == idx
        sel = jnp.where(hit, 1.0, sel)
        score = jnp.where(hit, -jnp.inf, score)
        if want_idx:
            idx_mat = jnp.where(out_lane == i, idx, idx_mat)
    return sel, idx_mat


def _adaln_kernel(c_ref, w_ref, b_ref, o_ref):
    o_ref[...] = _dot(c_ref[...].astype(BF16), w_ref[...]) + b_ref[...]


def _adaln(c_all, w_bf, b_row):
    rows, d = c_all.shape
    n = w_bf.shape[1]
    tn = 1536
    return pl.pallas_call(
        _adaln_kernel,
        grid=(n // tn,),
        in_specs=[pl.BlockSpec((rows, d), lambda j: (0, 0)),
                  pl.BlockSpec((d, tn), lambda j: (0, j)),
                  pl.BlockSpec((1, tn), lambda j: (0, j))],
        out_specs=pl.BlockSpec((rows, tn), lambda j: (0, j)),
        out_shape=jax.ShapeDtypeStruct((rows, n), F32),
        compiler_params=_cparams(("parallel",)),
        name="adaln",
    )(c_all, w_bf, b_row)


def _norm_mod_mm_kernel(*refs, has_gate):
    if has_gate:
        x_ref, g_ref, sc_ref, sh_ref, w_ref, wg_ref, o_ref, og_ref, h_ref = refs
    else:
        x_ref, g_ref, sc_ref, sh_ref, w_ref, o_ref, h_ref = refs

    @pl.when(pl.program_id(1) == 0)
    def _():
        x = x_ref[...]
        ms = jnp.mean(x * x, axis=-1, keepdims=True)
        y = x * lax.rsqrt(ms + EPS) * g_ref[...]
        h = (y * (1.0 + sc_ref[0]) + sh_ref[0]).astype(BF16)
        h_ref[...] = h
        if has_gate:
            og_ref[...] = _dot(h, wg_ref[...])

    o_ref[...] = _dot(h_ref[...], w_ref[...])


def _norm_mod_matmul(x2d, g_row, sc, sh, w_bf, wg_bf, *, tm, tn, rows_per_mod, name):
    rows, d = x2d.shape
    n = w_bf.shape[1]
    per_row = sc.shape[1] != 1
    if per_row:
        mod_spec = pl.BlockSpec((1, tm, d), lambda i, j: (0, i, 0))
    else:
        mod_spec = pl.BlockSpec((1, 1, d), lambda i, j: ((i * tm) // rows_per_mod, 0, 0))
    has_gate = wg_bf is not None
    in_specs = [pl.BlockSpec((tm, d), lambda i, j: (i, 0)),
                pl.BlockSpec((1, d), lambda i, j: (0, 0)),
                mod_spec, mod_spec,
                pl.BlockSpec((d, tn), lambda i, j: (0, j))]
    out_specs = [pl.BlockSpec((tm, tn), lambda i, j: (i, j))]
    out_shape = [jax.ShapeDtypeStruct((rows, n), F32)]
    args = [x2d, g_row, sc, sh, w_bf]
    if has_gate:
        ng = wg_bf.shape[1]
        in_specs.append(pl.BlockSpec((d, ng), lambda i, j: (0, 0)))
        out_specs.append(pl.BlockSpec((tm, ng), lambda i, j: (i, 0)))
        out_shape.append(jax.ShapeDtypeStruct((rows, ng), F32))
        args.append(wg_bf)
    res = pl.pallas_call(
        functools.partial(_norm_mod_mm_kernel, has_gate=has_gate),
        grid=(rows // tm, n // tn),
        in_specs=in_specs,
        out_specs=out_specs,
        out_shape=out_shape,
        scratch_shapes=[pltpu.VMEM((tm, d), BF16)],
        compiler_params=_cparams(("parallel", "arbitrary")),
        name=name,
    )(*args)
    return res if has_gate else res[0]


def _compress_kernel(tab_ref, pool_ref, wp_ref, pos_ref, w1_ref, b1_ref, w2_ref, o_ref, buf_ref, sem,
                     *, npages):
    step = pl.program_id(0) * N_KV + pl.program_id(1)
    nsteps = pl.num_programs(0) * N_KV
    slot = lax.rem(step, 2)
    nch = npages * (PAGE // CMP_STRIDE)

    def page_copy(st, sl, p):
        page = tab_ref[lax.div(st, N_KV) * npages + p]
        return pltpu.make_async_copy(pool_ref.at[page, :, lax.rem(st, N_KV), :],
                                     buf_ref.at[sl, pl.ds(pl.multiple_of(p * PAGE, PAGE), PAGE), :],
                                     sem.at[sl])

    def start_all(st, sl):
        def body(p, carry):
            page_copy(st, sl, p).start()
            return carry
        lax.fori_loop(0, npages, body, 0)

    def wait_all(st, sl):
        def body(p, carry):
            page_copy(st, sl, p).wait()
            return carry
        lax.fori_loop(0, npages, body, 0)

    @pl.when(step == 0)
    def _():
        start_all(0, 0)

    @pl.when(step + 1 < nsteps)
    def _():
        start_all(step + 1, 1 - slot)

    wait_all(step, slot)

    const = _dot(pos_ref[...], w1_ref[...])[0:1, :] + b1_ref[...]
    acc = None
    for s in range(CMP_STRIDE):
        a = buf_ref[slot, pl.ds(s, nch, stride=CMP_STRIDE), :]
        part = _dot(a.astype(BF16), wp_ref[s])
        acc = part if acc is None else acc + part
    hid = acc[:, :CMP_HID] + pltpu.roll(acc[:, CMP_HID:], nch - 1, 0) + const
    act = hid * _sigmoid(hid)
    out = _dot(act.astype(BF16), w2_ref[...])
    rows = lax.broadcasted_iota(jnp.int32, (nch, HEAD_DIM), 0)
    o_ref[0, 0] = jnp.where(rows < nch - 1, out, 0.0)


def _compress_paged(pool4, table_flat, nseq, npages, wts):
    wp, posb, w1b, b1r, w2b = wts
    nch = npages * (PAGE // CMP_STRIDE)
    grid_spec = pltpu.PrefetchScalarGridSpec(
        num_scalar_prefetch=1,
        grid=(nseq, N_KV),
        in_specs=[pl.BlockSpec(memory_space=pl.ANY),
                  pl.BlockSpec(wp.shape, lambda b, g, t: (0, 0, 0)),
                  pl.BlockSpec(posb.shape, lambda b, g, t: (0, 0)),
                  pl.BlockSpec(w1b.shape, lambda b, g, t: (0, 0)),
                  pl.BlockSpec(b1r.shape, lambda b, g, t: (0, 0)),
                  pl.BlockSpec(w2b.shape, lambda b, g, t: (0, 0))],
        out_specs=pl.BlockSpec((1, 1, nch, HEAD_DIM), lambda b, g, t: (b, g, 0, 0)),
        scratch_shapes=[pltpu.VMEM((2, npages * PAGE, HEAD_DIM), F32),
                        pltpu.SemaphoreType.DMA((2,))],
    )
    return pl.pallas_call(
        functools.partial(_compress_kernel, npages=npages),
        grid_spec=grid_spec,
        out_shape=jax.ShapeDtypeStruct((nseq, N_KV, nch, HEAD_DIM), F32),
        compiler_params=_cparams(("arbitrary", "arbitrary")),
        name="compress",
    )(table_flat, pool4, wp, posb, w1b, b1r, w2b)


def _compress_weights(pos, w1, b1, w2):
    w1r = w1.reshape(2, CMP_STRIDE, HEAD_DIM, CMP_HID)
    wp = jnp.concatenate([w1r[0], w1r[1]], axis=-1)
    posb = jnp.broadcast_to(pos.reshape(1, -1), (SUBLANES, CMP_BLK * HEAD_DIM))
    return (wp.astype(BF16), posb.astype(BF16), w1.astype(BF16), b1.reshape(1, -1), w2.astype(BF16))


def _prompt_attn_kernel(q_ref, kc_ref, vc_ref, ks_ref, vs_ref, kw_ref, vw_ref, ag_ref, ovl_ref, o_ref,
                        m_ref, l_ref, acc_ref, *, seq_len, n_slc, n_lanes):
    i = pl.program_id(2)
    s0 = i * Q_BLK
    rows = GROUP * Q_BLK
    q = (q_ref[0, 0].reshape(rows, HEAD_DIM).astype(F32) * SCALE).astype(BF16)

    def qpos_of(shape):
        return s0 + (lax.broadcasted_iota(jnp.int32, shape, 0) & (Q_BLK - 1))

    nch = kc_ref.shape[2]
    kc = kc_ref[0, 0].astype(BF16)
    vc = vc_ref[0, 0].astype(BF16)
    s = _dot_nt(q, kc)
    blk_end = lax.broadcasted_iota(jnp.int32, (rows, nch), 1) * CMP_STRIDE + (CMP_BLK - 1)
    mask = blk_end <= qpos_of((rows, nch))
    s, m = _softmax_parts(s, mask)
    e = jnp.where(mask, jnp.exp(s - m), 0.0)
    p = e / jnp.maximum(jnp.sum(e, axis=-1, keepdims=True), 1e-30)
    o_cmp = _dot(p.astype(BF16), vc)

    psum = p[0:Q_BLK]
    for r in range(1, GROUP):
        psum = psum + p[r * Q_BLK:(r + 1) * Q_BLK]
    p_hi = psum.astype(BF16)
    p_lo = (psum - p_hi.astype(F32)).astype(BF16)
    ovl = ovl_ref[...]
    imp = _dot(p_hi, ovl) + _dot(p_lo, ovl)
    j = lax.broadcasted_iota(jnp.int32, (Q_BLK, n_lanes), 1)
    qp = s0 + lax.broadcasted_iota(jnp.int32, (Q_BLK, n_lanes), 0)
    tb = lax.shift_right_logical(qp, 6)
    forced = (j == 0) | (j == tb) | (j == tb - 1)
    score = jnp.where(forced, BIG, jnp.where(j * SLC_BLK <= qp, imp, NEG))
    score = jnp.where(j < n_slc, score, -jnp.inf)
    sel, _ = _topk_select(score, j.astype(F32), min(TOPK, n_slc), False)
    sel_bf = sel.astype(BF16)

    m_ref[...] = jnp.full((rows, 1), NEG, F32)
    l_ref[...] = jnp.zeros((rows, 1), F32)
    acc_ref[...] = jnp.zeros((rows, HEAD_DIM), F32)
    n_tiles = (s0 + Q_BLK + KEY_TILE - 1) // KEY_TILE
    blocks_per_tile = KEY_TILE // SLC_BLK

    def tile_body(t, carry):
        k0 = pl.multiple_of(t * KEY_TILE, KEY_TILE)
        kt = ks_ref[0, 0, pl.ds(k0, KEY_TILE), :]
        vt = vs_ref[0, 0, pl.ds(k0, KEY_TILE), :]
        sc = _dot_nt(q, kt)
        blk_row = lax.broadcasted_iota(jnp.int32, (n_lanes, KEY_TILE), 0)
        blk_col = lax.shift_right_logical(lax.broadcasted_iota(jnp.int32, (n_lanes, KEY_TILE), 1), 6)
        expand = jnp.where(blk_row == blk_col + t * blocks_per_tile, 1.0, 0.0).astype(BF16)
        selx = _dot(sel_bf, expand)
        selx = jnp.concatenate([selx] * GROUP, axis=0)
        kpos = k0 + lax.broadcasted_iota(jnp.int32, (rows, KEY_TILE), 1)
        msk = jnp.where(kpos <= qpos_of((rows, KEY_TILE)), selx, 0.0) > 0.5
        sc = jnp.where(msk, sc, NEG)
        m_old = m_ref[...]
        m_new = jnp.maximum(m_old, jnp.max(sc, axis=-1, keepdims=True))
        pt = jnp.where(msk, jnp.exp(sc - m_new), 0.0)
        alpha = jnp.exp(m_old - m_new)
        l_ref[...] = alpha * l_ref[...] + jnp.sum(pt, axis=-1, keepdims=True)
        acc_ref[...] = alpha * acc_ref[...] + _dot(pt.astype(BF16), vt)
        m_ref[...] = m_new
        return carry

    lax.fori_loop(0, n_tiles, tile_body, 0)
    o_sel = acc_ref[...] / jnp.maximum(l_ref[...], 1e-30)

    w0 = pl.multiple_of(jnp.clip(s0 - WINDOW, 0, seq_len - WIN_KEYS), Q_BLK)
    kwt = kw_ref[0, 0, pl.ds(w0, WIN_KEYS), :]
    vwt = vw_ref[0, 0, pl.ds(w0, WIN_KEYS), :]
    sw = _dot_nt(q, kwt)
    kpos = w0 + lax.broadcasted_iota(jnp.int32, (rows, WIN_KEYS), 1)
    qpw = qpos_of((rows, WIN_KEYS))
    wmask = jnp.where(kpos <= qpw, jnp.where(kpos > qpw - WINDOW, 1.0, 0.0), 0.0) > 0.5
    sw, mw = _softmax_parts(sw, wmask)
    ew = jnp.where(wmask, jnp.exp(sw - mw), 0.0)
    pw = ew / jnp.maximum(jnp.sum(ew, axis=-1, keepdims=True), 1e-30)
    o_win = _dot(pw.astype(BF16), vwt)

    gate = _sigmoid(ag_ref[0])
    for r in range(GROUP):
        rs = slice(r * Q_BLK, (r + 1) * Q_BLK)
        y = (gate[:, r:r + 1] * o_cmp[rs]
             + gate[:, GROUP + r:GROUP + r + 1] * o_sel[rs]
             + gate[:, 2 * GROUP + r:2 * GROUP + r + 1] * o_win[rs])
        o_ref[0, 0, r] = y


def _prompt_attention(qh, kc, vc, ks, vs, kw, vw, zag, ovl, seq_len):
    bsz = qh.shape[0]
    nch = kc.shape[2]
    n_slc = seq_len // SLC_BLK
    n_lanes = ovl.shape[1]
    rows = GROUP * Q_BLK
    kv_spec = pl.BlockSpec((1, 1, seq_len, HEAD_DIM), lambda b, g, i: (b, g, 0, 0))
    c_spec = pl.BlockSpec((1, 1, nch, HEAD_DIM), lambda b, g, i: (b, g, 0, 0))
    return pl.pallas_call(
        functools.partial(_prompt_attn_kernel, seq_len=seq_len, n_slc=n_slc, n_lanes=n_lanes),
        grid=(bsz, N_KV, seq_len // Q_BLK),
        in_specs=[pl.BlockSpec((1, 1, GROUP, Q_BLK, HEAD_DIM), lambda b, g, i: (b, g, 0, i, 0)),
                  c_spec, c_spec, kv_spec, kv_spec, kv_spec, kv_spec,
                  pl.BlockSpec((1, Q_BLK, LANES), lambda b, g, i: (b, i, g)),
                  pl.BlockSpec(ovl.shape, lambda b, g, i: (0, 0))],
        out_specs=pl.BlockSpec((1, 1, GROUP, Q_BLK, HEAD_DIM), lambda b, g, i: (b, g, 0, i, 0)),
        out_shape=jax.ShapeDtypeStruct((bsz, N_KV, GROUP, seq_len, HEAD_DIM), F32),
        scratch_shapes=[pltpu.VMEM((rows, 1), F32), pltpu.VMEM((rows, 1), F32),
                        pltpu.VMEM((rows, HEAD_DIM), F32)],
        compiler_params=_cparams(("parallel", "parallel", "arbitrary")),
        name="prompt_attn",
    )(qh, kc, vc, ks, vs, kw, vw, zag, ovl)


def _overlap(n_rows, n_cmp, n_slc, n_lanes):
    i = np.arange(n_rows)[:, None] * CMP_STRIDE
    jj = np.arange(n_lanes)[None, :] * SLC_BLK
    ovl = (i <= jj + SLC_BLK - 1) & (i + CMP_BLK - 1 >= jj)
    ovl &= (np.arange(n_rows)[:, None] < n_cmp) & (np.arange(n_lanes)[None, :] < n_slc)
    return jnp.asarray(ovl.astype(np.float32), dtype=BF16)


def _sample_cmp_kernel(q_ref, kc_ref, vc_ref, ovl_ref, oc_ref, idx_ref, *, past, n_slc, n_lanes):
    nch = kc_ref.shape[2]
    rows = GROUP * SUBLANES
    ovl = ovl_ref[...]
    for g in range(N_KV):
        q = (q_ref[0, g].astype(F32) * SCALE).astype(BF16)
        kc = kc_ref[0, g].astype(BF16)
        vc = vc_ref[0, g].astype(BF16)
        s = _dot_nt(q, kc)
        qpos = past + (lax.broadcasted_iota(jnp.int32, (rows, nch), 0) & (SUBLANES - 1))
        blk_end = lax.broadcasted_iota(jnp.int32, (rows, nch), 1) * CMP_STRIDE + (CMP_BLK - 1)
        mask = blk_end <= qpos
        s, m = _softmax_parts(s, mask)
        e = jnp.where(mask, jnp.exp(s - m), 0.0)
        p = e / jnp.maximum(jnp.sum(e, axis=-1, keepdims=True), 1e-30)
        oc_ref[0, g] = _dot(p.astype(BF16), vc)
        psum = p[0:SUBLANES]
        for r in range(1, GROUP):
            psum = psum + p[r * SUBLANES:(r + 1) * SUBLANES]
        p_hi = psum.astype(BF16)
        p_lo = (psum - p_hi.astype(F32)).astype(BF16)
        imp = _dot(p_hi, ovl) + _dot(p_lo, ovl)
        j = lax.broadcasted_iota(jnp.int32, (SUBLANES, n_lanes), 1)
        qp = past + lax.broadcasted_iota(jnp.int32, (SUBLANES, n_lanes), 0)
        tb = lax.shift_right_logical(qp, 6)
        forced = (j == 0) | (j == tb) | (j == tb - 1)
        score = jnp.where(forced, BIG, jnp.where(j * SLC_BLK <= qp, imp, NEG))
        score = jnp.where(j < n_slc, score, -jnp.inf)
        _, idx = _topk_select(score, j.astype(F32), TOPK, True)
        idx_ref[0, g] = idx.astype(jnp.int32)


def _sample_cmp(q1, kc, vc, ovl, past, n_slc):
    ns = q1.shape[0]
    nch = kc.shape[2]
    rows = GROUP * SUBLANES
    return pl.pallas_call(
        functools.partial(_sample_cmp_kernel, past=past, n_slc=n_slc, n_lanes=ovl.shape[1]),
        grid=(ns,),
        in_specs=[pl.BlockSpec((1, N_KV, rows, HEAD_DIM), lambda b: (b, 0, 0, 0)),
                  pl.BlockSpec((1, N_KV, nch, HEAD_DIM), lambda b: (b, 0, 0, 0)),
                  pl.BlockSpec((1, N_KV, nch, HEAD_DIM), lambda b: (b, 0, 0, 0)),
                  pl.BlockSpec(ovl.shape, lambda b: (0, 0))],
        out_specs=[pl.BlockSpec((1, N_KV, rows, HEAD_DIM), lambda b: (b, 0, 0, 0)),
                   pl.BlockSpec((1, N_KV, SUBLANES, LANES), lambda b: (b, 0, 0, 0))],
        out_shape=[jax.ShapeDtypeStruct((ns, N_KV, rows, HEAD_DIM), F32),
                   jax.ShapeDtypeStruct((ns, N_KV, SUBLANES, LANES), jnp.int32)],
        compiler_params=_cparams(("parallel",)),
        name="sample_cmp",
    )(q1, kc, vc, ovl)


def _sample_selwin_kernel(prow_ref, sk_ref, sv_ref, wk_ref, wv_ref, idx_ref, q_ref, ksn_ref, vsn_ref,
                          kwn_ref, vwn_ref, oc_ref, ag_ref, o_ref, kbuf, vbuf, wkbuf, wvbuf, sem,
                          *, n_q, past, wb):
    b = pl.program_id(0)
    g = pl.program_id(1)
    step = b * N_KV + g
    nsteps = pl.num_programs(0) * N_KV
    slot = lax.rem(step, 2)
    pb = past // SLC_BLK
    nkeys = TOPK * SLC_BLK

    def copies(st, sl):
        sb = st // N_KV
        sg = lax.rem(st, N_KV)
        res = []
        for qi in range(n_q):
            for k in range(TOPK):
                pr = prow_ref[(st * n_q + qi) * TOPK + k]
                page = lax.shift_right_logical(pr, 1)
                r0 = pl.multiple_of((pr & 1) * SLC_BLK, SLC_BLK)
                dst = pl.ds(k * SLC_BLK, SLC_BLK)
                res.append(pltpu.make_async_copy(sk_ref.at[page, pl.ds(r0, SLC_BLK), sg, :],
                                                 kbuf.at[sl, qi, dst, :], sem.at[sl]))
                res.append(pltpu.make_async_copy(sv_ref.at[page, pl.ds(r0, SLC_BLK), sg, :],
                                                 vbuf.at[sl, qi, dst, :], sem.at[sl]))
        res.append(pltpu.make_async_copy(wk_ref.at[sb, :, sg, :], wkbuf.at[sl], sem.at[sl]))
        res.append(pltpu.make_async_copy(wv_ref.at[sb, :, sg, :], wvbuf.at[sl], sem.at[sl]))
        return res

    @pl.when(step == 0)
    def _():
        for cp in copies(0, 0):
            cp.start()

    @pl.when(step + 1 < nsteps)
    def _():
        for cp in copies(step + 1, 1 - slot):
            cp.start()

    for cp in copies(step, slot):
        cp.wait()

    rows = n_q * SUBLANES
    gate = _sigmoid(ag_ref[0, 0])
    ksn = ksn_ref[0, 0].astype(BF16)
    vsn = vsn_ref[0, 0].astype(BF16)
    new_t = lax.broadcasted_iota(jnp.int32, (SUBLANES, SUBLANES), 1)
    lane_k = lax.broadcasted_iota(jnp.int32, (SUBLANES, LANES), 1)
    blk_row = lax.broadcasted_iota(jnp.int32, (LANES, nkeys), 0)
    blk_col = lax.shift_right_logical(lax.broadcasted_iota(jnp.int32, (LANES, nkeys), 1), 6)
    expand = jnp.where(blk_row == blk_col, 1.0, 0.0).astype(BF16)
    idx_all = idx_ref[0, 0]

    o_sel_rows = []
    for qi in range(n_q):
        qq = (q_ref[0, 0, qi].astype(F32) * SCALE).astype(BF16)
        kg = kbuf[slot, qi].astype(BF16)
        vg = vbuf[slot, qi].astype(BF16)
        idx_row = jnp.broadcast_to(idx_all[qi:qi + 1, :], (SUBLANES, LANES))
        valid_k = jnp.where(lane_k < TOPK, 1.0, 0.0)
        in_past = jnp.where(idx_row < pb, valid_k, 0.0)
        is_new = jnp.where(idx_row == pb, valid_k, 0.0)
        has_new = jnp.max(is_new, axis=-1, keepdims=True)
        m1 = _dot(in_past.astype(BF16), expand) > 0.5
        s1 = jnp.where(m1, _dot_nt(qq, kg), NEG)
        m2 = jnp.where(new_t <= qi, jnp.where(new_t < n_q, has_new, 0.0), 0.0) > 0.5
        s2 = jnp.where(m2, _dot_nt(qq, ksn), NEG)
        mx = jnp.maximum(jnp.max(s1, axis=-1, keepdims=True), jnp.max(s2, axis=-1, keepdims=True))
        e1 = jnp.where(m1, jnp.exp(s1 - mx), 0.0)
        e2 = jnp.where(m2, jnp.exp(s2 - mx), 0.0)
        den = jnp.maximum(jnp.sum(e1, axis=-1, keepdims=True) + jnp.sum(e2, axis=-1, keepdims=True), 1e-30)
        o = _dot((e1 / den).astype(BF16), vg) + _dot((e2 / den).astype(BF16), vsn)
        o_sel_rows.append(o)
    o_sel = jnp.concatenate(o_sel_rows, axis=0)

    qa = (q_ref[0, 0].reshape(rows, HEAD_DIM).astype(F32) * SCALE).astype(BF16)
    kwc = wkbuf[slot].astype(BF16)
    vwc = wvbuf[slot].astype(BF16)
    kwn = kwn_ref[0, 0].astype(BF16)
    vwn = vwn_ref[0, 0].astype(BF16)
    qpos = past + lax.shift_right_logical(lax.broadcasted_iota(jnp.int32, (rows, wb), 0), 3)
    kpos = (past - wb) + lax.broadcasted_iota(jnp.int32, (rows, wb), 1)
    mw1 = jnp.where(kpos <= qpos, jnp.where(kpos > qpos - WINDOW, jnp.where(kpos >= 0, 1.0, 0.0), 0.0), 0.0) > 0.5
    sw1 = jnp.where(mw1, _dot_nt(qa, kwc), NEG)
    qi_n = lax.shift_right_logical(lax.broadcasted_iota(jnp.int32, (rows, SUBLANES), 0), 3)
    t_n = lax.broadcasted_iota(jnp.int32, (rows, SUBLANES), 1)
    mw2 = jnp.where(t_n <= qi_n, jnp.where(t_n < n_q, jnp.where(t_n > qi_n - WINDOW, 1.0, 0.0), 0.0), 0.0) > 0.5
    sw2 = jnp.where(mw2, _dot_nt(qa, kwn), NEG)
    mx = jnp.maximum(jnp.max(sw1, axis=-1, keepdims=True), jnp.max(sw2, axis=-1, keepdims=True))
    e1 = jnp.where(mw1, jnp.exp(sw1 - mx), 0.0)
    e2 = jnp.where(mw2, jnp.exp(sw2 - mx), 0.0)
    den = jnp.maximum(jnp.sum(e1, axis=-1, keepdims=True) + jnp.sum(e2, axis=-1, keepdims=True), 1e-30)
    o_win = _dot((e1 / den).astype(BF16), vwc) + _dot((e2 / den).astype(BF16), vwn)

    o_cmp = oc_ref[0, 0].reshape(rows, HEAD_DIM)
    y = gate[:, 0:1] * o_cmp + gate[:, 1:2] * o_sel + gate[:, 2:3] * o_win
    o_ref[0, 0] = y.reshape(n_q, SUBLANES, HEAD_DIM)


def _sample_selwin(prow, slc_k, slc_v, win_k, win_v, idx, q2, ksn, vsn, kwn, vwn, oc2, ag2, past):
    ns, _, n_q = q2.shape[:3]
    wb = win_k.shape[1]
    rows = n_q * SUBLANES
    nkeys = TOPK * SLC_BLK
    any_spec = pl.BlockSpec(memory_space=pl.ANY)
    new_spec = pl.BlockSpec((1, 1, SUBLANES, HEAD_DIM), lambda b, g, pr: (b, g, 0, 0))
    qspec = pl.BlockSpec((1, 1, n_q, SUBLANES, HEAD_DIM), lambda b, g, pr: (b, g, 0, 0, 0))
    grid_spec = pltpu.PrefetchScalarGridSpec(
        num_scalar_prefetch=1,
        grid=(ns, N_KV),
        in_specs=[any_spec, any_spec, any_spec, any_spec,
                  pl.BlockSpec((1, 1, SUBLANES, LANES), lambda b, g, pr: (b, g, 0, 0)),
                  qspec, new_spec, new_spec, new_spec, new_spec, qspec,
                  pl.BlockSpec((1, 1, rows, LANES), lambda b, g, pr: (b, g, 0, 0))],
        out_specs=qspec,
        scratch_shapes=[pltpu.VMEM((2, n_q, nkeys, HEAD_DIM), F32),
                        pltpu.VMEM((2, n_q, nkeys, HEAD_DIM), F32),
                        pltpu.VMEM((2, wb, HEAD_DIM), F32),
                        pltpu.VMEM((2, wb, HEAD_DIM), F32),
                        pltpu.SemaphoreType.DMA((2,))],
    )
    return pl.pallas_call(
        functools.partial(_sample_selwin_kernel, n_q=n_q, past=past, wb=wb),
        grid_spec=grid_spec,
        out_shape=jax.ShapeDtypeStruct((ns, N_KV, n_q, SUBLANES, HEAD_DIM), F32),
        compiler_params=_cparams(("arbitrary", "arbitrary")),
        name="sample_selwin",
    )(prow, slc_k, slc_v, win_k, win_v, idx, q2, ksn, vsn, kwn, vwn, oc2, ag2)


def _win_shift_kernel(w_ref, n_ref, o_ref, *, n_new):
    wb = w_ref.shape[1]
    o_ref[0, 0:wb - n_new] = w_ref[0, n_new:wb]
    o_ref[0, wb - n_new:wb] = n_ref[0]


def _win_shift(win, new):
    ns, wb = win.shape[:2]
    n_new = new.shape[1]
    return pl.pallas_call(
        functools.partial(_win_shift_kernel, n_new=n_new),
        grid=(ns,),
        in_specs=[pl.BlockSpec((1, wb, N_KV, HEAD_DIM), lambda b: (b, 0, 0, 0)),
                  pl.BlockSpec((1, n_new, N_KV, HEAD_DIM), lambda b: (b, 0, 0, 0))],
        out_specs=pl.BlockSpec((1, wb, N_KV, HEAD_DIM), lambda b: (b, 0, 0, 0)),
        out_shape=jax.ShapeDtypeStruct(win.shape, F32),
        compiler_params=_cparams(("parallel",)),
        name="win_shift",
    )(win, new)


def _merge_project(bg, conv, mg, yb, x, gt, wout_ref):
    ya = bg * conv
    c = ya.shape[-1]
    mix = _sigmoid(mg[..., :c]) * ya + _sigmoid(mg[..., c:]) * yb
    shp = mix.shape
    proj = _dot(mix.reshape(-1, c).astype(BF16), wout_ref[...]).reshape(shp[:-1] + (wout_ref.shape[1],))
    return x + gt * proj


def _mixer_long_kernel(bg_ref, cg_ref, xin_ref, mg_ref, yb_ref, x_ref, gt_ref, hc_ref, hx_ref, cw_ref,
                       wout_ref, o_ref, tail_ref):
    u = cg_ref[...] * xin_ref[...]
    tm = u.shape[0]
    up = hc_ref[0, 0] * hx_ref[0, 0]
    row = lax.broadcasted_iota(jnp.int32, u.shape, 0)
    u1 = jnp.where(row == 0, up[1:2], pltpu.roll(u, 1, 0))
    u2 = jnp.where(row == 0, up[0:1], jnp.where(row == 1, up[1:2], pltpu.roll(u, 2, 0)))
    cw = cw_ref[...]
    conv = cw[0:1] * u2 + cw[1:2] * u1 + cw[2:3] * u
    o_ref[0] = _merge_project(bg_ref[...], conv, mg_ref[...], yb_ref[0], x_ref[0], gt_ref[0], wout_ref)
    tail_ref[0, 0] = u[tm - 2:tm]


def _mixer_long(z, yb, x, gt, halo_c, halo_x, conv_w, wout_bf, *, tm):
    bsz, seq, c = x.shape
    nt = seq // tm
    row = lambda b, i: b * nt + i
    zspec = lambda col: pl.BlockSpec((tm, c), lambda b, i, col=col: (row(b, i), col))
    return pl.pallas_call(
        _mixer_long_kernel,
        grid=(bsz, nt),
        in_specs=[zspec(COL_BG // c), zspec(COL_CG // c), zspec(COL_XIN // c),
                  pl.BlockSpec((tm, 2 * c), lambda b, i: (row(b, i), COL_MG // (2 * c))),
                  pl.BlockSpec((1, tm, c), lambda b, i: (b, i, 0)),
                  pl.BlockSpec((1, tm, c), lambda b, i: (b, i, 0)),
                  pl.BlockSpec((1, 1, c), lambda b, i: (b, 0, 0)),
                  pl.BlockSpec((1, 1, 2, c), lambda b, i: (b, i, 0, 0)),
                  pl.BlockSpec((1, 1, 2, c), lambda b, i: (b, i, 0, 0)),
                  pl.BlockSpec(conv_w.shape, lambda b, i: (0, 0)),
                  pl.BlockSpec(wout_bf.shape, lambda b, i: (0, 0))],
        out_specs=[pl.BlockSpec((1, tm, c), lambda b, i: (b, i, 0)),
                   pl.BlockSpec((1, 1, 2, c), lambda b, i: (b, i, 0, 0))],
        out_shape=[jax.ShapeDtypeStruct((bsz, seq, c), F32),
                   jax.ShapeDtypeStruct((bsz, nt, 2, c), F32)],
        compiler_params=_cparams(("parallel", "parallel")),
        name="mixer_long",
    )(z, z, z, z, yb, x, gt, halo_c, halo_x, conv_w, wout_bf)


def _mixer_short_kernel(bg_ref, cg_ref, xin_ref, mg_ref, yb_ref, x_ref, gt_ref, prev_ref, cw_ref, wout_ref,
                        o_ref, st_ref):
    u = cg_ref[...] * xin_ref[...]
    t_len = u.shape[0]
    ext = [prev_ref[0], prev_ref[1]] + [u[t] for t in range(t_len)]
    cw = cw_ref[...]
    conv = jnp.stack([cw[0:1] * ext[t] + cw[1:2] * ext[t + 1] + cw[2:3] * ext[t + 2] for t in range(t_len)])
    o_ref[...] = _merge_project(bg_ref[...], conv, mg_ref[...], yb_ref[...], x_ref[...], gt_ref[...], wout_ref)
    st_ref[0] = ext[t_len]
    st_ref[1] = ext[t_len + 1]


def _mixer_short(z3, yb3, x3, gt, prev, conv_w, wout_bf):
    t_len, ns, c = x3.shape
    zspec = lambda col: pl.BlockSpec((t_len, ns, c), lambda i, col=col: (0, 0, col))
    full3 = pl.BlockSpec((t_len, ns, c), lambda i: (0, 0, 0))
    return pl.pallas_call(
        _mixer_short_kernel,
        grid=(1,),
        in_specs=[zspec(COL_BG // c), zspec(COL_CG // c), zspec(COL_XIN // c),
                  pl.BlockSpec((t_len, ns, 2 * c), lambda i: (0, 0, COL_MG // (2 * c))),
                  full3, full3,
                  pl.BlockSpec((1, ns, c), lambda i: (0, 0, 0)),
                  pl.BlockSpec((2, ns, c), lambda i: (0, 0, 0)),
                  pl.BlockSpec(conv_w.shape, lambda i: (0, 0)),
                  pl.BlockSpec(wout_bf.shape, lambda i: (0, 0))],
        out_specs=[full3, pl.BlockSpec((2, ns, c), lambda i: (0, 0, 0))],
        out_shape=[jax.ShapeDtypeStruct((t_len, ns, c), F32), jax.ShapeDtypeStruct((2, ns, c), F32)],
        compiler_params=_cparams(("arbitrary",)),
        name="mixer_short",
    )(z3, z3, z3, z3, yb3, x3, gt, prev, conv_w, wout_bf)


def _ffn_finish(act, x, gt, wd_ref, gf_ref):
    shp = act.shape
    proj = _dot(act.reshape(-1, shp[-1]).astype(BF16), wd_ref[...]).reshape(shp[:-1] + (wd_ref.shape[1],))
    x2 = x + gt * proj
    ms = jnp.mean(x2 * x2, axis=-1, keepdims=True)
    return x2 * lax.rsqrt(ms + EPS) * gf_ref[...]


def _ffn_long_kernel(u_ref, v_ref, x_ref, gt_ref, hu_ref, cw_ref, wd_ref, gf_ref, o_ref):
    u = u_ref[...]
    up = hu_ref[0, 0]
    row = lax.broadcasted_iota(jnp.int32, u.shape, 0)
    u1 = jnp.where(row == 0, up[1:2], pltpu.roll(u, 1, 0))
    u2 = jnp.where(row == 0, up[0:1], jnp.where(row == 1, up[1:2], pltpu.roll(u, 2, 0)))
    cw = cw_ref[...]
    uc = cw[0:1] * u2 + cw[1:2] * u1 + cw[2:3] * u
    act = uc * _sigmoid(uc) * v_ref[...]
    o_ref[0] = _ffn_finish(act, x_ref[0], gt_ref[0], wd_ref, gf_ref)


def _ffn_long(uv, x1, gt, halo_u, conv_w, wd_bf, gf_row, *, tm):
    bsz, seq, d = x1.shape
    f = conv_w.shape[1]
    nt = seq // tm
    row = lambda b, i: b * nt + i
    return pl.pallas_call(
        _ffn_long_kernel,
        grid=(bsz, nt),
        in_specs=[pl.BlockSpec((tm, f), lambda b, i: (row(b, i), 0)),
                  pl.BlockSpec((tm, f), lambda b, i: (row(b, i), 1)),
                  pl.BlockSpec((1, tm, d), lambda b, i: (b, i, 0)),
                  pl.BlockSpec((1, 1, d), lambda b, i: (b, 0, 0)),
                  pl.BlockSpec((1, 1, 2, f), lambda b, i: (b, i, 0, 0)),
                  pl.BlockSpec(conv_w.shape, lambda b, i: (0, 0)),
                  pl.BlockSpec(wd_bf.shape, lambda b, i: (0, 0)),
                  pl.BlockSpec(gf_row.shape, lambda b, i: (0, 0))],
        out_specs=pl.BlockSpec((1, tm, d), lambda b, i: (b, i, 0)),
        out_shape=jax.ShapeDtypeStruct((bsz, seq, d), F32),
        compiler_params=_cparams(("parallel", "parallel")),
        name="ffn_long",
    )(uv, uv, x1, gt, halo_u, conv_w, wd_bf, gf_row)


def _ffn_short_kernel(u_ref, v_ref, x_ref, gt_ref, prev_ref, cw_ref, wd_ref, gf_ref, o_ref):
    u = u_ref[...]
    t_len = u.shape[0]
    ext = [prev_ref[0], prev_ref[1]] + [u[t] for t in range(t_len)]
    cw = cw_ref[...]
    uc = jnp.stack([cw[0:1] * ext[t] + cw[1:2] * ext[t + 1] + cw[2:3] * ext[t + 2] for t in range(t_len)])
    act = uc * _sigmoid(uc) * v_ref[...]
    o_ref[...] = _ffn_finish(act, x_ref[...], gt_ref[...], wd_ref, gf_ref)


def _ffn_short(uv3, x3, gt, prev, conv_w, wd_bf, gf_row):
    t_len, ns, d = x3.shape
    f = conv_w.shape[1]
    return pl.pallas_call(
        _ffn_short_kernel,
        grid=(1,),
        in_specs=[pl.BlockSpec((t_len, ns, f), lambda i: (0, 0, 0)),
                  pl.BlockSpec((t_len, ns, f), lambda i: (0, 0, 1)),
                  pl.BlockSpec((t_len, ns, d), lambda i: (0, 0, 0)),
                  pl.BlockSpec((1, ns, d), lambda i: (0, 0, 0)),
                  pl.BlockSpec((2, ns, f), lambda i: (0, 0, 0)),
                  pl.BlockSpec(conv_w.shape, lambda i: (0, 0)),
                  pl.BlockSpec(wd_bf.shape, lambda i: (0, 0)),
                  pl.BlockSpec(gf_row.shape, lambda i: (0, 0))],
        out_specs=pl.BlockSpec((t_len, ns, d), lambda i: (0, 0, 0)),
        out_shape=jax.ShapeDtypeStruct((t_len, ns, d), F32),
        compiler_params=_cparams(("arbitrary",)),
        name="ffn_short",
    )(uv3, uv3, x3, gt, prev, conv_w, wd_bf, gf_row)


def _pad_axis(a, axis, size):
    pad = [(0, 0)] * a.ndim
    pad[axis] = (0, size - a.shape[axis])
    return jnp.pad(a, pad)


def _split_weights(w_in):
    w_main = jnp.concatenate([w_in[:, 0:IN_OFF_KV],
                              w_in[:, IN_OFF_MG:IN_OFF_MG + 2 * D_MODEL],
                              w_in[:, IN_OFF_KV:IN_OFF_AG]], axis=1)
    src = np.zeros((AG_W,), np.int32)
    keep = np.zeros((AG_W,), np.float32)
    for g in range(N_KV):
        for c in range(3):
            for r in range(GROUP):
                src[g * LANES + c * GROUP + r] = IN_OFF_AG + c * N_KV * GROUP + g * GROUP + r
                keep[g * LANES + c * GROUP + r] = 1.0
    w_ag = w_in[:, src] * keep[None, :]
    return w_main.astype(BF16), w_ag.astype(BF16)


def kernel(x_prompt, x_sample, cache_cmp_k, cache_cmp_v, cache_slc_k, cache_slc_v, cache_win_k, cache_win_v,
           state_conv, state_ffn, page_table, c_prompt, c_sample, w_ada, b_ada, norm1_g, norm2_g, w_in, conv_w,
           cmp_pos_k, cmp_w1_k, cmp_b1_k, cmp_w2_k, cmp_pos_v, cmp_w1_v, cmp_b1_v, cmp_w2_v,
           w_out, w_up, ffn_conv_w, w_down, norm_f_g):
    depth = w_ada.shape[0]
    assert depth == 1, "single trunk layer"
    bsz, seq, d = x_prompt.shape
    ns, n_q, _ = x_sample.shape
    npages = page_table.shape[1]
    past = npages * PAGE
    wb = cache_win_k.shape[2]
    assert d == D_MODEL and seq % KEY_TILE == 0 and seq >= WIN_KEYS and n_q <= SUBLANES and past % SLC_BLK == 0

    w_main, w_ag = _split_weights(w_in[0])
    w_ada_bf = w_ada[0].astype(BF16)
    w_out_bf = w_out[0].astype(BF16)
    w_up_bf = w_up[0].astype(BF16)
    w_down_bf = w_down[0].astype(BF16)
    g1 = norm1_g[0].reshape(1, d)
    g2 = norm2_g[0].reshape(1, d)
    gf = norm_f_g.reshape(1, d)
    cw = conv_w[0]
    fcw = ffn_conv_w[0]
    wts_k = _compress_weights(cmp_pos_k[0], cmp_w1_k[0], cmp_b1_k[0], cmp_w2_k[0])
    wts_v = _compress_weights(cmp_pos_v[0], cmp_w1_v[0], cmp_b1_v[0], cmp_w2_v[0])

    n_c = bsz + ns
    n_c_pad = -(-n_c // SUBLANES) * SUBLANES
    c_all = _pad_axis(jnp.concatenate([c_prompt, c_sample], axis=0), 0, n_c_pad)
    mod = _adaln(c_all, w_ada_bf, b_ada[0].reshape(1, -1))
    mod_p = mod[:bsz].reshape(bsz, 6, 1, d)
    mod_s = mod[bsz:n_c].reshape(ns, 6, d)
    sh1_p, sc1_p, gt1_p, sh2_p, sc2_p, gt2_p = (mod_p[:, k] for k in range(6))
    sh1_s, sc1_s, gt1_s, sh2_s, sc2_s, gt2_s = (mod_s[:, k] for k in range(6))

    def kv_slices(z2d):
        return [z2d[:, COL_KV + k * KV_W:COL_KV + (k + 1) * KV_W] for k in range(6)]

    rows_p = bsz * seq
    tm_p = min(1024, seq)
    z_p, zag_p = _norm_mod_matmul(x_prompt.reshape(rows_p, d), g1, sc1_p, sh1_p, w_main, w_ag,
                                  tm=tm_p, tn=1280, rows_per_mod=seq, name="in_proj_prompt")
    kv_p = [a.reshape(bsz, seq, N_KV, HEAD_DIM) for a in kv_slices(z_p)]
    tab_p = jnp.arange(bsz * (seq // PAGE), dtype=jnp.int32)
    kc_p = _compress_paged(kv_p[0].reshape(-1, PAGE, N_KV, HEAD_DIM), tab_p, bsz, seq // PAGE, wts_k)
    vc_p = _compress_paged(kv_p[1].reshape(-1, PAGE, N_KV, HEAD_DIM), tab_p, bsz, seq // PAGE, wts_v)
    heads = [a.transpose(0, 2, 1, 3).astype(BF16) for a in kv_p[2:]]
    qh = z_p[:, COL_Q:COL_Q + d].reshape(bsz, seq, N_KV, GROUP, HEAD_DIM).transpose(0, 2, 3, 1, 4).astype(BF16)
    n_slc_p = seq // SLC_BLK
    n_lanes_p = -(-n_slc_p // LANES) * LANES
    nch_p = seq // CMP_STRIDE
    ovl_p = _overlap(nch_p, nch_p - 1, n_slc_p, n_lanes_p)
    yb_h = _prompt_attention(qh, kc_p, vc_p, heads[0], heads[1], heads[2], heads[3],
                             zag_p.reshape(bsz, seq, AG_W), ovl_p, seq)
    yb_p = yb_h.transpose(0, 3, 1, 2, 4).reshape(bsz, seq, d)

    tm_m = min(512, seq)
    nt_m = seq // tm_m
    z4 = z_p.reshape(bsz, nt_m, tm_m, Z_W)
    conv0 = jnp.zeros((bsz, 1, 2, d), F32)
    halo_c = jnp.concatenate([conv0, z4[:, :-1, tm_m - 2:, COL_CG:COL_CG + d]], axis=1)
    halo_x = jnp.concatenate([jnp.ones((bsz, 1, 2, d), F32), z4[:, :-1, tm_m - 2:, COL_XIN:COL_XIN + d]], axis=1)
    x1_p, tail_p = _mixer_long(z_p, yb_p, x_prompt, gt1_p, halo_c, halo_x, cw, w_out_bf, tm=tm_m)
    conv_p = tail_p[:, -1]

    uv_p = _norm_mod_matmul(x1_p.reshape(rows_p, d), g2, sc2_p, sh2_p, w_up_bf, None,
                            tm=tm_p, tn=1408, rows_per_mod=seq, name="up_proj_prompt")
    tm_f = min(256, seq)
    nt_f = seq // tm_f
    u4 = uv_p.reshape(bsz, nt_f, tm_f, 2 * D_FF)
    halo_u = jnp.concatenate([jnp.zeros((bsz, 1, 2, D_FF), F32), u4[:, :-1, tm_f - 2:, :D_FF]], axis=1)
    y_prompt = _ffn_long(uv_p, x1_p, gt2_p, halo_u, fcw, w_down_bf, gf, tm=tm_f)
    ffn_p = u4[:, -1, tm_f - 2:, :D_FF]

    wbp = min(WINDOW, seq)
    outs_p = [a[None] for a in kv_p[:4]] + [kv_p[4][None, :, seq - wbp:], kv_p[5][None, :, seq - wbp:],
                                            conv_p[None], ffn_p[None]]

    rows_s = n_q * ns
    x_s3 = x_sample.transpose(1, 0, 2)
    tile_t = lambda a: jnp.tile(a, (n_q, 1))[None]
    z_s, zag_s = _norm_mod_matmul(x_s3.reshape(rows_s, d), g1, tile_t(sc1_s), tile_t(sh1_s), w_main, w_ag,
                                  tm=rows_s, tn=1280, rows_per_mod=rows_s, name="in_proj_sample")
    kv_s = [a.reshape(n_q, ns, N_KV, HEAD_DIM).transpose(1, 0, 2, 3) for a in kv_slices(z_s)]
    pool = lambda c: c[0]
    tab_s = page_table.reshape(-1).astype(jnp.int32)
    kc_s = _compress_paged(pool(cache_cmp_k), tab_s, ns, npages, wts_k)
    vc_s = _compress_paged(pool(cache_cmp_v), tab_s, ns, npages, wts_v)

    q_s = z_s[:, COL_Q:COL_Q + d].reshape(n_q, ns, N_KV, GROUP, HEAD_DIM)
    q1 = _pad_axis(q_s.transpose(1, 2, 3, 0, 4), 3, SUBLANES).reshape(ns, N_KV, GROUP * SUBLANES, HEAD_DIM)
    q2 = _pad_axis(q_s.transpose(1, 2, 0, 3, 4), 3, SUBLANES)
    n_slc_s = -(-(past + n_q) // SLC_BLK)
    n_lanes_s = -(-n_slc_s // LANES) * LANES
    nch_s = past // CMP_STRIDE
    ovl_s = _overlap(nch_s, nch_s - 1, n_slc_s, n_lanes_s)
    oc1, idx = _sample_cmp(q1.astype(BF16), kc_s, vc_s, ovl_s, past, n_slc_s)
    oc2 = _pad_axis(oc1.reshape(ns, N_KV, GROUP, SUBLANES, HEAD_DIM)[:, :, :, :n_q].transpose(0, 1, 3, 2, 4),
                    3, SUBLANES)

    pbk = past // SLC_BLK
    bpp = PAGE // SLC_BLK
    jp = jnp.minimum(idx[:, :, :n_q, :TOPK], pbk - 1)
    page = jnp.take_along_axis(page_table, (jp // bpp).reshape(ns, -1), axis=1).reshape(jp.shape)
    prow = (page * bpp + jp % bpp).reshape(-1).astype(jnp.int32)

    new_rows = [_pad_axis(a.transpose(0, 2, 1, 3), 2, SUBLANES) for a in kv_s[2:]]
    ag_s = zag_s.reshape(n_q, ns, N_KV, LANES)[..., :3 * GROUP].reshape(n_q, ns, N_KV, 3, GROUP)
    ag2 = _pad_axis(_pad_axis(ag_s.transpose(1, 2, 0, 4, 3), 3, SUBLANES), 4, LANES)
    ag2 = ag2.reshape(ns, N_KV, n_q * SUBLANES, LANES)
    y_h = _sample_selwin(prow, pool(cache_slc_k), pool(cache_slc_v), pool(cache_win_k), pool(cache_win_v),
                         idx, q2.astype(BF16), new_rows[0], new_rows[1], new_rows[2], new_rows[3], oc2, ag2, past)
    yb_s3 = y_h[:, :, :, :GROUP].transpose(2, 0, 1, 3, 4).reshape(n_q, ns, d)

    win_k_s = _win_shift(pool(cache_win_k), kv_s[4])
    win_v_s = _win_shift(pool(cache_win_v), kv_s[5])

    z_s3 = z_s.reshape(n_q, ns, Z_W)
    x1_s3, conv_s3 = _mixer_short(z_s3, yb_s3, x_s3, gt1_s[None], state_conv[0].transpose(1, 0, 2), cw, w_out_bf)
    uv_s = _norm_mod_matmul(x1_s3.reshape(rows_s, d), g2, tile_t(sc2_s), tile_t(sh2_s), w_up_bf, None,
                            tm=rows_s, tn=1408, rows_per_mod=rows_s, name="up_proj_sample")
    uv_s3 = uv_s.reshape(n_q, ns, 2 * D_FF)
    y_s3 = _ffn_short(uv_s3, x1_s3, gt2_s[None], state_ffn[0].transpose(1, 0, 2), fcw, w_down_bf, gf)
    y_sample = y_s3.transpose(1, 0, 2)
    ffn_ext = jnp.concatenate([state_ffn[0], uv_s3[:, :, :D_FF].transpose(1, 0, 2)], axis=1)
    ffn_s = ffn_ext[:, n_q:]
    conv_s = conv_s3.transpose(1, 0, 2)

    outs_s = [a[None] for a in kv_s[:4]] + [win_k_s[None], win_v_s[None], conv_s[None], ffn_s[None]]
    return (y_prompt, y_sample, *outs_p, *outs_s)
```

```python
import functools

import numpy as np
import jax
import jax.numpy as jnp
from jax import lax
from jax.experimental import pallas as pl
from jax.experimental.pallas import tpu as pltpu

F32 = jnp.float32
BF16 = jnp.bfloat16

D_MODEL = 1024
N_KV = 4
GROUP = 4
HEAD_DIM = 64
KV_W = N_KV * HEAD_DIM
GPAIRS = N_KV // 2
CMP_BLK = 32
CMP_STRIDE = 16
CMP_HID = 128
SLC_BLK = 64
TOPK = 16
WINDOW = 512
PAGE = 128
D_FF = 2816
EPS = 1e-6
NEG = -1e30
BIG = 1e30
SCALE = HEAD_DIM ** -0.5

LANES = 128
SUBLANES = 8
VMEM_LIMIT = 56 * 1024 * 1024

COL_BG, COL_CG, COL_XIN, COL_Q, COL_MG, COL_KV = 0, 1024, 2048, 3072, 4096, 6144
Z_W = 7680
AG_W = N_KV * LANES
IN_OFF_Q = 3072
IN_OFF_KV = 4096
IN_OFF_AG = 5632
IN_OFF_MG = 5680

KEY_TILE = 512
WIN_KEYS = WINDOW + 128
Q_BLK = 128


def _cparams(sem, vmem=VMEM_LIMIT):
    return pltpu.CompilerParams(dimension_semantics=sem, vmem_limit_bytes=vmem)


def _dot(a, b):
    return jnp.dot(a, b, preferred_element_type=F32)


def _dot_nt(a, b):
    return lax.dot_general(a, b, (((1,), (1,)), ((), ())), preferred_element_type=F32)


def _sigmoid(x):
    return 1.0 / (1.0 + jnp.exp(-x))


def _softmax_parts(s, mask):
    s = jnp.where(mask, s, NEG)
    m = jnp.max(s, axis=-1, keepdims=True)
    return s, m


def _topk_select(score, pos, k, axis, want_idx=False):
    n = score.shape[axis]
    sel = jnp.zeros(score.shape, F32)
    idx_mat = jnp.zeros((score.shape[0], LANES), F32) if want_idx else None
    out_lane = lax.broadcasted_iota(jnp.int32, (score.shape[0], LANES), 1) if want_idx else None
    for i in range(k):
        m = jnp.max(score, axis=axis, keepdims=True)
        cand = jnp.where(score == m, pos, float(n))
        idx = jnp.min(cand, axis=axis, keepdims=True)
        hit = pos == idx
        sel = jnp.where(hit, 1.0, sel)
        score = jnp.where(hit, -jnp.inf, score)
        if want_idx:
            idx_mat = jnp.where(out_lane == i, idx, idx_mat)
    return sel, idx_mat


def _adaln_kernel(c_ref, w_ref, b_ref, o_ref):
    o_ref[...] = _dot(c_ref[...].astype(BF16), w_ref[...]) + b_ref[...]


def _adaln(c_all, w_bf, b_row):
    rows, d = c_all.shape
    n = w_bf.shape[1]
    tn = 1536
    return pl.pallas_call(
        _adaln_kernel,
        grid=(n // tn,),
        in_specs=[pl.BlockSpec((rows, d), lambda j: (0, 0)),
                  pl.BlockSpec((d, tn), lambda j: (0, j)),
                  pl.BlockSpec((1, tn), lambda j: (0, j))],
        out_specs=pl.BlockSpec((rows, tn), lambda j: (0, j)),
        out_shape=jax.ShapeDtypeStruct((rows, n), F32),
        compiler_params=_cparams(("parallel",)),
        name="adaln",
    )(c_all, w_bf, b_row)


def _norm_mod_mm_kernel(*refs, has_gate):
    if has_gate:
        x_ref, g_ref, sc_ref, sh_ref, w_ref, wg_ref, o_ref, og_ref, h_ref = refs
    else:
        x_ref, g_ref, sc_ref, sh_ref, w_ref, o_ref, h_ref = refs

    @pl.when(pl.program_id(1) == 0)
    def _():
        x = x_ref[...]
        ms = jnp.mean(x * x, axis=-1, keepdims=True)
        y = x * lax.rsqrt(ms + EPS) * g_ref[...]
        h = (y * (1.0 + sc_ref[0]) + sh_ref[0]).astype(BF16)
        h_ref[...] = h
        if has_gate:
            og_ref[...] = _dot(h, wg_ref[...])

    o_ref[...] = _dot(h_ref[...], w_ref[...])


def _norm_mod_matmul(x2d, g_row, sc, sh, w_bf, wg_bf, *, tm, tn, rows_per_mod, name):
    rows, d = x2d.shape
    n = w_bf.shape[1]
    per_row = sc.shape[1] != 1
    if per_row:
        mod_spec = pl.BlockSpec((1, tm, d), lambda i, j: (0, i, 0))
    else:
        mod_spec = pl.BlockSpec((1, 1, d), lambda i, j: ((i * tm) // rows_per_mod, 0, 0))
    has_gate = wg_bf is not None
    in_specs = [pl.BlockSpec((tm, d), lambda i, j: (i, 0)),
                pl.BlockSpec((1, d), lambda i, j: (0, 0)),
                mod_spec, mod_spec,
                pl.BlockSpec((d, tn), lambda i, j: (0, j))]
    out_specs = [pl.BlockSpec((tm, tn), lambda i, j: (i, j))]
    out_shape = [jax.ShapeDtypeStruct((rows, n), F32)]
    args = [x2d, g_row, sc, sh, w_bf]
    if has_gate:
        ng = wg_bf.shape[1]
        in_specs.append(pl.BlockSpec((d, ng), lambda i, j: (0, 0)))
        out_specs.append(pl.BlockSpec((tm, ng), lambda i, j: (i, 0)))
        out_shape.append(jax.ShapeDtypeStruct((rows, ng), F32))
        args.append(wg_bf)
    res = pl.pallas_call(
        functools.partial(_norm_mod_mm_kernel, has_gate=has_gate),
        grid=(rows // tm, n // tn),
        in_specs=in_specs,
        out_specs=out_specs,
        out_shape=out_shape,
        scratch_shapes=[pltpu.VMEM((tm, d), BF16)],
        compiler_params=_cparams(("parallel", "arbitrary")),
        name=name,
    )(*args)
    return res if has_gate else res[0]


def _compress_kernel(tab_ref, pool_ref, wp_ref, pos_ref, w1_ref, b1_ref, w2_ref, o_ref, xt_ref, x_ref, sem,
                     *, npages):
    step = pl.program_id(0) * GPAIRS + pl.program_id(1)
    nsteps = pl.num_programs(0) * GPAIRS
    slot = lax.rem(step, 2)
    nch = npages * (PAGE // CMP_STRIDE)
    ntok = npages * PAGE

    def page_copy(st, sl, p):
        page = tab_ref[lax.div(st, GPAIRS) * npages + p]
        return pltpu.make_async_copy(pool_ref.at[0, page, lax.rem(st, GPAIRS)],
                                     xt_ref.at[sl, :, pl.ds(pl.multiple_of(p * PAGE, PAGE), PAGE)],
                                     sem.at[sl])

    def start_all(st, sl):
        def body(p, carry):
            page_copy(st, sl, p).start()
            return carry
        lax.fori_loop(0, npages, body, 0)

    def wait_all(st, sl):
        def body(p, carry):
            page_copy(st, sl, p).wait()
            return carry
        lax.fori_loop(0, npages, body, 0)

    @pl.when(step == 0)
    def _():
        start_all(0, 0)

    @pl.when(step + 1 < nsteps)
    def _():
        start_all(step + 1, 1 - slot)

    wait_all(step, slot)

    slab = min(ntok, 512)
    for j in range(ntok // slab):
        x_ref[pl.ds(j * slab, slab), :] = xt_ref[slot, :, pl.ds(j * slab, slab)].T

    const = _dot(pos_ref[...], w1_ref[...])[0:1, :] + b1_ref[...]
    acc = None
    for s in range(CMP_STRIDE):
        a = x_ref[pl.ds(s, nch, stride=CMP_STRIDE), :]
        part = _dot(a.astype(BF16), wp_ref[s])
        acc = part if acc is None else acc + part
    rows = lax.broadcasted_iota(jnp.int32, (nch, HEAD_DIM), 0)
    for gi in range(2):
        first = acc[:, gi * 2 * CMP_HID:gi * 2 * CMP_HID + CMP_HID]
        second = acc[:, gi * 2 * CMP_HID + CMP_HID:(gi + 1) * 2 * CMP_HID]
        hid = first + pltpu.roll(second, nch - 1, 0) + const
        act = hid * _sigmoid(hid)
        out = _dot(act.astype(BF16), w2_ref[...])
        o_ref[0, gi] = jnp.where(rows < nch - 1, out, 0.0)


def _compress_paged(pool_t, table_flat, nseq, npages, wts):
    wp, posb, w1b, b1r, w2b = wts
    nch = npages * (PAGE // CMP_STRIDE)
    ntok = npages * PAGE
    pool_pairs = pool_t.reshape(1, -1, GPAIRS, 2 * HEAD_DIM, PAGE)
    grid_spec = pltpu.PrefetchScalarGridSpec(
        num_scalar_prefetch=1,
        grid=(nseq, GPAIRS),
        in_specs=[pl.BlockSpec(memory_space=pl.ANY),
                  pl.BlockSpec(wp.shape, lambda b, g, t: (0, 0, 0)),
                  pl.BlockSpec(posb.shape, lambda b, g, t: (0, 0)),
                  pl.BlockSpec(w1b.shape, lambda b, g, t: (0, 0)),
                  pl.BlockSpec(b1r.shape, lambda b, g, t: (0, 0)),
                  pl.BlockSpec(w2b.shape, lambda b, g, t: (0, 0))],
        out_specs=pl.BlockSpec((1, 2, nch, HEAD_DIM), lambda b, g, t: (b, g, 0, 0)),
        scratch_shapes=[pltpu.VMEM((2, 2 * HEAD_DIM, ntok), F32),
                        pltpu.VMEM((ntok, 2 * HEAD_DIM), F32),
                        pltpu.SemaphoreType.DMA((2,))],
    )
    return pl.pallas_call(
        functools.partial(_compress_kernel, npages=npages),
        grid_spec=grid_spec,
        out_shape=jax.ShapeDtypeStruct((nseq, N_KV, nch, HEAD_DIM), F32),
        compiler_params=_cparams(("arbitrary", "arbitrary")),
        name="compress",
    )(table_flat, pool_pairs, wp, posb, w1b, b1r, w2b)


def _compress_weights(pos, w1, b1, w2):
    w1r = w1.reshape(2, CMP_STRIDE, HEAD_DIM, CMP_HID)
    cat = jnp.concatenate([w1r[0], w1r[1]], axis=-1)
    zero = jnp.zeros_like(cat)
    wp = jnp.concatenate([jnp.concatenate([cat, zero], axis=-1), jnp.concatenate([zero, cat], axis=-1)], axis=1)
    posb = jnp.broadcast_to(pos.reshape(1, -1), (SUBLANES, CMP_BLK * HEAD_DIM))
    return (wp.astype(BF16), posb.astype(BF16), w1.astype(BF16), b1.reshape(1, -1), w2.astype(BF16))


def _prompt_attn_kernel(q_ref, kc_ref, vc_ref, ka_ref, vs_ref, kw_ref, vw_ref, ag_ref, ovl_ref, o_ref,
                        m_ref, l_ref, acc_ref, *, seq_len, n_slc, n_lanes):
    i = pl.program_id(2)
    s0 = i * Q_BLK
    rows = GROUP * Q_BLK
    q = (q_ref[0, 0].reshape(rows, HEAD_DIM).astype(F32) * SCALE).astype(BF16)

    def qpos_of(shape):
        return s0 + (lax.broadcasted_iota(jnp.int32, shape, 0) & (Q_BLK - 1))

    nch = kc_ref.shape[2]
    kc = kc_ref[0, 0].astype(BF16)
    vc = vc_ref[0, 0].astype(BF16)
    s = _dot_nt(q, kc)
    blk_end = lax.broadcasted_iota(jnp.int32, (rows, nch), 1) * CMP_STRIDE + (CMP_BLK - 1)
    mask = blk_end <= qpos_of((rows, nch))
    s, m = _softmax_parts(s, mask)
    e = jnp.where(mask, jnp.exp(s - m), 0.0)
    p = e / jnp.maximum(jnp.sum(e, axis=-1, keepdims=True), 1e-30)
    o_cmp = _dot(p.astype(BF16), vc)

    psum = p[0:Q_BLK]
    for r in range(1, GROUP):
        psum = psum + p[r * Q_BLK:(r + 1) * Q_BLK]
    p_hi = psum.astype(BF16)
    p_lo = (psum - p_hi.astype(F32)).astype(BF16)
    ovl_t = ovl_ref[...]
    imp_t = _dot_nt(ovl_t, p_hi) + _dot_nt(ovl_t, p_lo)
    j = lax.broadcasted_iota(jnp.int32, (n_lanes, Q_BLK), 0)
    qp = s0 + lax.broadcasted_iota(jnp.int32, (n_lanes, Q_BLK), 1)
    tb = lax.shift_right_logical(qp, 6)
    forced = (j == 0) | (j == tb) | (j == tb - 1)
    score = jnp.where(forced, BIG, jnp.where(j * SLC_BLK <= qp, imp_t, NEG))
    score = jnp.where(j < n_slc, score, -jnp.inf)
    sel_t, _ = _topk_select(score, j.astype(F32), min(TOPK, n_slc), 0)
    bias = jnp.where(sel_t.T > 0.5, 0.0, NEG).astype(BF16)
    q_aug = jnp.concatenate([jnp.concatenate([bias] * GROUP, axis=0), q], axis=1)

    m_ref[...] = jnp.full((rows, 1), NEG, F32)
    l_ref[...] = jnp.zeros((rows, 1), F32)
    acc_ref[...] = jnp.zeros((rows, HEAD_DIM), F32)

    def tile(t, causal):
        k0 = pl.multiple_of(t * KEY_TILE, KEY_TILE)
        sc = _dot_nt(q_aug, ka_ref[0, 0, pl.ds(k0, KEY_TILE), :])
        if causal:
            kpos = k0 + lax.broadcasted_iota(jnp.int32, (rows, KEY_TILE), 1)
            sc = jnp.where(kpos <= qpos_of((rows, KEY_TILE)), sc, NEG)
        m_old = m_ref[...]
        m_new = jnp.maximum(m_old, jnp.max(sc, axis=-1, keepdims=True))
        pt = jnp.exp(sc - m_new)
        alpha = jnp.exp(m_old - m_new)
        l_ref[...] = alpha * l_ref[...] + jnp.sum(pt, axis=-1, keepdims=True)
        acc_ref[...] = alpha * acc_ref[...] + _dot(pt.astype(BF16), vs_ref[0, 0, pl.ds(k0, KEY_TILE), :])
        m_ref[...] = m_new

    def full_tile(t, carry):
        tile(t, False)
        return carry

    n_full = s0 // KEY_TILE
    lax.fori_loop(0, n_full, full_tile, 0)
    tile(n_full, True)
    o_sel = acc_ref[...] / jnp.maximum(l_ref[...], 1e-30)

    w0 = pl.multiple_of(jnp.clip(s0 - WINDOW, 0, seq_len - WIN_KEYS), Q_BLK)
    kwt = kw_ref[0, 0, pl.ds(w0, WIN_KEYS), :]
    vwt = vw_ref[0, 0, pl.ds(w0, WIN_KEYS), :]
    sw = _dot_nt(q, kwt)
    kpos = w0 + lax.broadcasted_iota(jnp.int32, (rows, WIN_KEYS), 1)
    qpw = qpos_of((rows, WIN_KEYS))
    wmask = jnp.where(kpos <= qpw, jnp.where(kpos > qpw - WINDOW, 1.0, 0.0), 0.0) > 0.5
    sw, mw = _softmax_parts(sw, wmask)
    ew = jnp.where(wmask, jnp.exp(sw - mw), 0.0)
    pw = ew / jnp.maximum(jnp.sum(ew, axis=-1, keepdims=True), 1e-30)
    o_win = _dot(pw.astype(BF16), vwt)

    gate = _sigmoid(ag_ref[0])
    for r in range(GROUP):
        rs = slice(r * Q_BLK, (r + 1) * Q_BLK)
        y = (gate[:, r:r + 1] * o_cmp[rs]
             + gate[:, GROUP + r:GROUP + r + 1] * o_sel[rs]
             + gate[:, 2 * GROUP + r:2 * GROUP + r + 1] * o_win[rs])
        o_ref[0, 0, r] = y


def _prompt_attention(qh, kc, vc, ks, vs, kw, vw, zag, ovl, seq_len):
    bsz = qh.shape[0]
    nch = kc.shape[2]
    n_slc = seq_len // SLC_BLK
    n_lanes = ovl.shape[0]
    rows = GROUP * Q_BLK
    blk = jnp.arange(seq_len, dtype=jnp.int32)[:, None] // SLC_BLK
    onehot = (blk == jnp.arange(n_lanes, dtype=jnp.int32)[None, :]).astype(BF16)
    ka = jnp.concatenate([jnp.broadcast_to(onehot, ks.shape[:2] + onehot.shape), ks], axis=-1)
    kv_spec = pl.BlockSpec((1, 1, seq_len, HEAD_DIM), lambda b, g, i: (b, g, 0, 0))
    ka_spec = pl.BlockSpec((1, 1, seq_len, n_lanes + HEAD_DIM), lambda b, g, i: (b, g, 0, 0))
    c_spec = pl.BlockSpec((1, 1, nch, HEAD_DIM), lambda b, g, i: (b, g, 0, 0))
    ks = ka
    return pl.pallas_call(
        functools.partial(_prompt_attn_kernel, seq_len=seq_len, n_slc=n_slc, n_lanes=n_lanes),
        grid=(bsz, N_KV, seq_len // Q_BLK),
        in_specs=[pl.BlockSpec((1, 1, GROUP, Q_BLK, HEAD_DIM), lambda b, g, i: (b, g, 0, i, 0)),
                  c_spec, c_spec, ka_spec, kv_spec, kv_spec, kv_spec,
                  pl.BlockSpec((1, Q_BLK, LANES), lambda b, g, i: (b, i, g)),
                  pl.BlockSpec(ovl.shape, lambda b, g, i: (0, 0))],
        out_specs=pl.BlockSpec((1, 1, GROUP, Q_BLK, HEAD_DIM), lambda b, g, i: (b, g, 0, i, 0)),
        out_shape=jax.ShapeDtypeStruct((bsz, N_KV, GROUP, seq_len, HEAD_DIM), F32),
        scratch_shapes=[pltpu.VMEM((rows, 1), F32), pltpu.VMEM((rows, 1), F32),
                        pltpu.VMEM((rows, HEAD_DIM), F32)],
        compiler_params=_cparams(("parallel", "parallel", "arbitrary")),
        name="prompt_attn",
    )(qh, kc, vc, ks, vs, kw, vw, zag, ovl)


def _overlap(n_rows, n_cmp, n_slc, n_lanes):
    i = np.arange(n_rows)[:, None] * CMP_STRIDE
    jj = np.arange(n_lanes)[None, :] * SLC_BLK
    ovl = (i <= jj + SLC_BLK - 1) & (i + CMP_BLK - 1 >= jj)
    ovl &= (np.arange(n_rows)[:, None] < n_cmp) & (np.arange(n_lanes)[None, :] < n_slc)
    return jnp.asarray(ovl.astype(np.float32), dtype=BF16)


def _sample_cmp_kernel(q_ref, kc_ref, vc_ref, ovl_ref, oc_ref, idx_ref, *, past, n_slc, n_lanes):
    nch = kc_ref.shape[2]
    rows = GROUP * SUBLANES
    ovl = ovl_ref[...]
    imps = []
    for g in range(N_KV):
        q = (q_ref[0, g].astype(F32) * SCALE).astype(BF16)
        kc = kc_ref[0, g].astype(BF16)
        vc = vc_ref[0, g].astype(BF16)
        s = _dot_nt(q, kc)
        qpos = past + (lax.broadcasted_iota(jnp.int32, (rows, nch), 0) & (SUBLANES - 1))
        blk_end = lax.broadcasted_iota(jnp.int32, (rows, nch), 1) * CMP_STRIDE + (CMP_BLK - 1)
        mask = blk_end <= qpos
        s, m = _softmax_parts(s, mask)
        e = jnp.where(mask, jnp.exp(s - m), 0.0)
        p = e / jnp.maximum(jnp.sum(e, axis=-1, keepdims=True), 1e-30)
        oc_ref[0, g] = _dot(p.astype(BF16), vc)
        psum = p[0:SUBLANES]
        for r in range(1, GROUP):
            psum = psum + p[r * SUBLANES:(r + 1) * SUBLANES]
        p_hi = psum.astype(BF16)
        p_lo = (psum - p_hi.astype(F32)).astype(BF16)
        imps.append(_dot(p_hi, ovl) + _dot(p_lo, ovl))
    imp = jnp.concatenate(imps, axis=0)
    sel_rows = N_KV * SUBLANES
    j = lax.broadcasted_iota(jnp.int32, (sel_rows, n_lanes), 1)
    qp = past + (lax.broadcasted_iota(jnp.int32, (sel_rows, n_lanes), 0) & (SUBLANES - 1))
    tb = lax.shift_right_logical(qp, 6)
    forced = (j == 0) | (j == tb) | (j == tb - 1)
    score = jnp.where(forced, BIG, jnp.where(j * SLC_BLK <= qp, imp, NEG))
    score = jnp.where(j < n_slc, score, -jnp.inf)
    _, idx = _topk_select(score, j.astype(F32), TOPK, 1, True)
    idx_ref[0] = idx.astype(jnp.int32).reshape(N_KV, SUBLANES, LANES)


def _sample_cmp(q1, kc, vc, ovl, past, n_slc):
    ns = q1.shape[0]
    nch = kc.shape[2]
    rows = GROUP * SUBLANES
    return pl.pallas_call(
        functools.partial(_sample_cmp_kernel, past=past, n_slc=n_slc, n_lanes=ovl.shape[1]),
        grid=(ns,),
        in_specs=[pl.BlockSpec((1, N_KV, rows, HEAD_DIM), lambda b: (b, 0, 0, 0)),
                  pl.BlockSpec((1, N_KV, nch, HEAD_DIM), lambda b: (b, 0, 0, 0)),
                  pl.BlockSpec((1, N_KV, nch, HEAD_DIM), lambda b: (b, 0, 0, 0)),
                  pl.BlockSpec(ovl.shape, lambda b: (0, 0))],
        out_specs=[pl.BlockSpec((1, N_KV, rows, HEAD_DIM), lambda b: (b, 0, 0, 0)),
                   pl.BlockSpec((1, N_KV, SUBLANES, LANES), lambda b: (b, 0, 0, 0))],
        out_shape=[jax.ShapeDtypeStruct((ns, N_KV, rows, HEAD_DIM), F32),
                   jax.ShapeDtypeStruct((ns, N_KV, SUBLANES, LANES), jnp.int32)],
        compiler_params=_cparams(("parallel",)),
        name="sample_cmp",
    )(q1, kc, vc, ovl)


def _sample_selwin_kernel(prow_ref, sk_ref, sv_ref, wk_ref, wv_ref, idx_ref, q_ref, ksn_ref, vsn_ref,
                          kwn_ref, vwn_ref, oc_ref, ag_ref, o_ref, kbuf, vbuf, sem,
                          *, n_q, past, wb):
    b = pl.program_id(0)
    g = pl.program_id(1)
    step = b * N_KV + g
    nsteps = pl.num_programs(0) * N_KV
    slot = lax.rem(step, 2)
    pb = past // SLC_BLK
    nkeys = TOPK * PAGE

    def copies(st, sl):
        sg = lax.rem(st, N_KV)
        res = []
        for qi in range(n_q):
            for k in range(TOPK):
                page = lax.shift_right_logical(prow_ref[(st * n_q + qi) * TOPK + k], 1)
                dst = pl.ds(k * PAGE, PAGE)
                res.append(pltpu.make_async_copy(sk_ref.at[0, page, sg], kbuf.at[sl, qi, :, dst], sem.at[sl]))
                res.append(pltpu.make_async_copy(sv_ref.at[0, page, sg], vbuf.at[sl, qi, :, dst], sem.at[sl]))
        return res

    @pl.when(step == 0)
    def _():
        for cp in copies(0, 0):
            cp.start()

    @pl.when(step + 1 < nsteps)
    def _():
        for cp in copies(step + 1, 1 - slot):
            cp.start()

    for cp in copies(step, slot):
        cp.wait()

    rows = n_q * SUBLANES
    gate = _sigmoid(ag_ref[0, 0])
    ksn = ksn_ref[0, 0].astype(BF16)
    vsn = vsn_ref[0, 0].astype(BF16)
    new_t = lax.broadcasted_iota(jnp.int32, (SUBLANES, SUBLANES), 1)
    lane_k = lax.broadcasted_iota(jnp.int32, (SUBLANES, LANES), 1)
    slot_row = lax.broadcasted_iota(jnp.int32, (LANES, nkeys), 0)
    key_col = lax.broadcasted_iota(jnp.int32, (LANES, nkeys), 1)
    same_slot = slot_row == lax.shift_right_logical(key_col, 7)
    upper = (lax.shift_right_logical(key_col, 6) & 1) == 1
    expand_lo = jnp.where(same_slot, jnp.where(upper, 0.0, 1.0), 0.0).astype(BF16)
    expand_hi = jnp.where(same_slot, jnp.where(upper, 1.0, 0.0), 0.0).astype(BF16)
    idx_all = idx_ref[0, 0]

    o_sel_rows = []
    for qi in range(n_q):
        qq = (q_ref[0, 0, qi].astype(F32) * SCALE).astype(BF16)
        kg = kbuf[slot, qi].astype(BF16)
        vg = vbuf[slot, qi].astype(BF16)
        idx_row = jnp.broadcast_to(idx_all[qi:qi + 1, :], (SUBLANES, LANES))
        valid_k = jnp.where(lane_k < TOPK, 1.0, 0.0)
        in_past = jnp.where(idx_row < pb, valid_k, 0.0)
        odd = (jnp.minimum(idx_row, pb - 1) & 1) == 1
        is_new = jnp.where(idx_row == pb, valid_k, 0.0)
        has_new = jnp.max(is_new, axis=-1, keepdims=True)
        sel_lo = jnp.where(odd, 0.0, in_past).astype(BF16)
        sel_hi = jnp.where(odd, in_past, 0.0).astype(BF16)
        m1 = (_dot(sel_lo, expand_lo) + _dot(sel_hi, expand_hi)) > 0.5
        s1 = jnp.where(m1, _dot(qq, kg), NEG)
        m2 = jnp.where(new_t <= qi, jnp.where(new_t < n_q, has_new, 0.0), 0.0) > 0.5
        s2 = jnp.where(m2, _dot_nt(qq, ksn), NEG)
        mx = jnp.maximum(jnp.max(s1, axis=-1, keepdims=True), jnp.max(s2, axis=-1, keepdims=True))
        e1 = jnp.where(m1, jnp.exp(s1 - mx), 0.0)
        e2 = jnp.where(m2, jnp.exp(s2 - mx), 0.0)
        den = jnp.maximum(jnp.sum(e1, axis=-1, keepdims=True) + jnp.sum(e2, axis=-1, keepdims=True), 1e-30)
        o = _dot_nt((e1 / den).astype(BF16), vg) + _dot((e2 / den).astype(BF16), vsn)
        o_sel_rows.append(o)
    o_sel = jnp.concatenate(o_sel_rows, axis=0)

    qa = (q_ref[0, 0].reshape(rows, HEAD_DIM).astype(F32) * SCALE).astype(BF16)
    kwc = wk_ref[0, 0, 0].astype(BF16)
    vwc = wv_ref[0, 0, 0].astype(BF16)
    kwn = kwn_ref[0, 0].astype(BF16)
    vwn = vwn_ref[0, 0].astype(BF16)
    qpos = past + lax.shift_right_logical(lax.broadcasted_iota(jnp.int32, (rows, wb), 0), 3)
    kpos = (past - wb) + lax.broadcasted_iota(jnp.int32, (rows, wb), 1)
    mw1 = jnp.where(kpos <= qpos, jnp.where(kpos > qpos - WINDOW, jnp.where(kpos >= 0, 1.0, 0.0), 0.0), 0.0) > 0.5
    sw1 = jnp.where(mw1, _dot(qa, kwc), NEG)
    qi_n = lax.shift_right_logical(lax.broadcasted_iota(jnp.int32, (rows, SUBLANES), 0), 3)
    t_n = lax.broadcasted_iota(jnp.int32, (rows, SUBLANES), 1)
    mw2 = jnp.where(t_n <= qi_n, jnp.where(t_n < n_q, jnp.where(t_n > qi_n - WINDOW, 1.0, 0.0), 0.0), 0.0) > 0.5
    sw2 = jnp.where(mw2, _dot_nt(qa, kwn), NEG)
    mx = jnp.maximum(jnp.max(sw1, axis=-1, keepdims=True), jnp.max(sw2, axis=-1, keepdims=True))
    e1 = jnp.where(mw1, jnp.exp(sw1 - mx), 0.0)
    e2 = jnp.where(mw2, jnp.exp(sw2 - mx), 0.0)
    den = jnp.maximum(jnp.sum(e1, axis=-1, keepdims=True) + jnp.sum(e2, axis=-1, keepdims=True), 1e-30)
    o_win = _dot_nt((e1 / den).astype(BF16), vwc) + _dot((e2 / den).astype(BF16), vwn)

    o_cmp = oc_ref[0, 0].reshape(rows, HEAD_DIM)
    y = gate[:, 0:1] * o_cmp + gate[:, 1:2] * o_sel + gate[:, 2:3] * o_win
    o_ref[0, 0] = y.reshape(n_q, SUBLANES, HEAD_DIM)


def _sample_selwin(prow, slc_k, slc_v, win_k, win_v, idx, q2, ksn, vsn, kwn, vwn, oc2, ag2, past):
    ns, _, n_q = q2.shape[:3]
    wb = win_k.shape[-1]
    rows = n_q * SUBLANES
    nkeys = TOPK * PAGE
    any_spec = pl.BlockSpec(memory_space=pl.ANY)
    win_spec = pl.BlockSpec((1, 1, 1, HEAD_DIM, wb), lambda b, g, pr: (0, b, g, 0, 0))
    new_spec = pl.BlockSpec((1, 1, SUBLANES, HEAD_DIM), lambda b, g, pr: (b, g, 0, 0))
    qspec = pl.BlockSpec((1, 1, n_q, SUBLANES, HEAD_DIM), lambda b, g, pr: (b, g, 0, 0, 0))
    grid_spec = pltpu.PrefetchScalarGridSpec(
        num_scalar_prefetch=1,
        grid=(ns, N_KV),
        in_specs=[any_spec, any_spec, win_spec, win_spec,
                  pl.BlockSpec((1, 1, SUBLANES, LANES), lambda b, g, pr: (b, g, 0, 0)),
                  qspec, new_spec, new_spec, new_spec, new_spec, qspec,
                  pl.BlockSpec((1, 1, rows, LANES), lambda b, g, pr: (b, g, 0, 0))],
        out_specs=qspec,
        scratch_shapes=[pltpu.VMEM((2, n_q, HEAD_DIM, nkeys), F32),
                        pltpu.VMEM((2, n_q, HEAD_DIM, nkeys), F32),
                        pltpu.SemaphoreType.DMA((2,))],
    )
    return pl.pallas_call(
        functools.partial(_sample_selwin_kernel, n_q=n_q, past=past, wb=wb),
        grid_spec=grid_spec,
        out_shape=jax.ShapeDtypeStruct((ns, N_KV, n_q, SUBLANES, HEAD_DIM), F32),
        compiler_params=_cparams(("arbitrary", "arbitrary")),
        name="sample_selwin",
    )(prow, slc_k, slc_v, win_k, win_v, idx, q2, ksn, vsn, kwn, vwn, oc2, ag2)


def _win_shift_kernel(w_ref, n_ref, o_ref, *, n_new):
    wb = w_ref.shape[-1]
    lane = lax.broadcasted_iota(jnp.int32, (HEAD_DIM, LANES), 1)
    for g in range(N_KV):
        shifted = pltpu.roll(w_ref[0, 0, g], wb - n_new, 1)
        o_ref[0, 0, g] = shifted
        o_ref[0, 0, g, :, wb - LANES:wb] = jnp.where(lane >= LANES - n_new, n_ref[0, g], shifted[:, wb - LANES:wb])


def _win_shift(win_t, new_t, n_new):
    ns, wb = win_t.shape[1], win_t.shape[-1]
    return pl.pallas_call(
        functools.partial(_win_shift_kernel, n_new=n_new),
        grid=(ns,),
        in_specs=[pl.BlockSpec((1, 1, N_KV, HEAD_DIM, wb), lambda b: (0, b, 0, 0, 0)),
                  pl.BlockSpec((1, N_KV, HEAD_DIM, LANES), lambda b: (b, 0, 0, 0))],
        out_specs=pl.BlockSpec((1, 1, N_KV, HEAD_DIM, wb), lambda b: (0, b, 0, 0, 0)),
        out_shape=jax.ShapeDtypeStruct(win_t.shape, F32),
        compiler_params=_cparams(("parallel",)),
        name="win_shift",
    )(win_t, new_t)


def _merge_project(bg, conv, mg, yb, x, gt, wout_ref):
    ya = bg * conv
    c = ya.shape[-1]
    mix = _sigmoid(mg[..., :c]) * ya + _sigmoid(mg[..., c:]) * yb
    shp = mix.shape
    proj = _dot(mix.reshape(-1, c).astype(BF16), wout_ref[...]).reshape(shp[:-1] + (wout_ref.shape[1],))
    return x + gt * proj


def _mixer_long_kernel(bg_ref, cg_ref, xin_ref, mg_ref, yb_ref, x_ref, gt_ref, hc_ref, hx_ref, cw_ref,
                       wout_ref, o_ref, tail_ref):
    u = cg_ref[...] * xin_ref[...]
    tm = u.shape[0]
    up = hc_ref[0, 0] * hx_ref[0, 0]
    row = lax.broadcasted_iota(jnp.int32, u.shape, 0)
    u1 = jnp.where(row == 0, up[1:2], pltpu.roll(u, 1, 0))
    u2 = jnp.where(row == 0, up[0:1], jnp.where(row == 1, up[1:2], pltpu.roll(u, 2, 0)))
    cw = cw_ref[...]
    conv = cw[0:1] * u2 + cw[1:2] * u1 + cw[2:3] * u
    o_ref[0] = _merge_project(bg_ref[...], conv, mg_ref[...], yb_ref[0], x_ref[0], gt_ref[0], wout_ref)
    tail_ref[0, 0] = u[tm - 2:tm]


def _mixer_long(z, yb, x, gt, halo_c, halo_x, conv_w, wout_bf, *, tm):
    bsz, seq, c = x.shape
    nt = seq // tm
    row = lambda b, i: b * nt + i
    zspec = lambda col: pl.BlockSpec((tm, c), lambda b, i, col=col: (row(b, i), col))
    return pl.pallas_call(
        _mixer_long_kernel,
        grid=(bsz, nt),
        in_specs=[zspec(COL_BG // c), zspec(COL_CG // c), zspec(COL_XIN // c),
                  pl.BlockSpec((tm, 2 * c), lambda b, i: (row(b, i), COL_MG // (2 * c))),
                  pl.BlockSpec((1, tm, c), lambda b, i: (b, i, 0)),
                  pl.BlockSpec((1, tm, c), lambda b, i: (b, i, 0)),
                  pl.BlockSpec((1, 1, c), lambda b, i: (b, 0, 0)),
                  pl.BlockSpec((1, 1, 2, c), lambda b, i: (b, i, 0, 0)),
                  pl.BlockSpec((1, 1, 2, c), lambda b, i: (b, i, 0, 0)),
                  pl.BlockSpec(conv_w.shape, lambda b, i: (0, 0)),
                  pl.BlockSpec(wout_bf.shape, lambda b, i: (0, 0))],
        out_specs=[pl.BlockSpec((1, tm, c), lambda b, i: (b, i, 0)),
                   pl.BlockSpec((1, 1, 2, c), lambda b, i: (b, i, 0, 0))],
        out_shape=[jax.ShapeDtypeStruct((bsz, seq, c), F32),
                   jax.ShapeDtypeStruct((bsz, nt, 2, c), F32)],
        compiler_params=_cparams(("parallel", "parallel")),
        name="mixer_long",
    )(z, z, z, z, yb, x, gt, halo_c, halo_x, conv_w, wout_bf)


def _mixer_short_kernel(bg_ref, cg_ref, xin_ref, mg_ref, yb_ref, x_ref, gt_ref, prev_ref, cw_ref, wout_ref,
                        o_ref, st_ref):
    u = cg_ref[...] * xin_ref[...]
    t_len = u.shape[0]
    ext = [prev_ref[0], prev_ref[1]] + [u[t] for t in range(t_len)]
    cw = cw_ref[...]
    conv = jnp.stack([cw[0:1] * ext[t] + cw[1:2] * ext[t + 1] + cw[2:3] * ext[t + 2] for t in range(t_len)])
    o_ref[...] = _merge_project(bg_ref[...], conv, mg_ref[...], yb_ref[...], x_ref[...], gt_ref[...], wout_ref)
    st_ref[0] = ext[t_len]
    st_ref[1] = ext[t_len + 1]


def _mixer_short(z3, yb3, x3, gt, prev, conv_w, wout_bf):
    t_len, ns, c = x3.shape
    zspec = lambda col: pl.BlockSpec((t_len, ns, c), lambda i, col=col: (0, 0, col))
    full3 = pl.BlockSpec((t_len, ns, c), lambda i: (0, 0, 0))
    return pl.pallas_call(
        _mixer_short_kernel,
        grid=(1,),
        in_specs=[zspec(COL_BG // c), zspec(COL_CG // c), zspec(COL_XIN // c),
                  pl.BlockSpec((t_len, ns, 2 * c), lambda i: (0, 0, COL_MG // (2 * c))),
                  full3, full3,
                  pl.BlockSpec((1, ns, c), lambda i: (0, 0, 0)),
                  pl.BlockSpec((2, ns, c), lambda i: (0, 0, 0)),
                  pl.BlockSpec(conv_w.shape, lambda i: (0, 0)),
                  pl.BlockSpec(wout_bf.shape, lambda i: (0, 0))],
        out_specs=[full3, pl.BlockSpec((2, ns, c), lambda i: (0, 0, 0))],
        out_shape=[jax.ShapeDtypeStruct((t_len, ns, c), F32), jax.ShapeDtypeStruct((2, ns, c), F32)],
        compiler_params=_cparams(("arbitrary",)),
        name="mixer_short",
    )(z3, z3, z3, z3, yb3, x3, gt, prev, conv_w, wout_bf)


def _ffn_finish(act, x, gt, wd_ref, gf_ref):
    shp = act.shape
    proj = _dot(act.reshape(-1, shp[-1]).astype(BF16), wd_ref[...]).reshape(shp[:-1] + (wd_ref.shape[1],))
    x2 = x + gt * proj
    ms = jnp.mean(x2 * x2, axis=-1, keepdims=True)
    return x2 * lax.rsqrt(ms + EPS) * gf_ref[...]


def _ffn_long_kernel(u_ref, v_ref, x_ref, gt_ref, hu_ref, cw_ref, wd_ref, gf_ref, o_ref):
    u = u_ref[...]
    up = hu_ref[0, 0]
    row = lax.broadcasted_iota(jnp.int32, u.shape, 0)
    u1 = jnp.where(row == 0, up[1:2], pltpu.roll(u, 1, 0))
    u2 = jnp.where(row == 0, up[0:1], jnp.where(row == 1, up[1:2], pltpu.roll(u, 2, 0)))
    cw = cw_ref[...]
    uc = cw[0:1] * u2 + cw[1:2] * u1 + cw[2:3] * u
    act = uc * _sigmoid(uc) * v_ref[...]
    o_ref[0] = _ffn_finish(act, x_ref[0], gt_ref[0], wd_ref, gf_ref)


def _ffn_long(uv, x1, gt, halo_u, conv_w, wd_bf, gf_row, *, tm):
    bsz, seq, d = x1.shape
    f = conv_w.shape[1]
    nt = seq // tm
    row = lambda b, i: b * nt + i
    return pl.pallas_call(
        _ffn_long_kernel,
        grid=(bsz, nt),
        in_specs=[pl.BlockSpec((tm, f), lambda b, i: (row(b, i), 0)),
                  pl.BlockSpec((tm, f), lambda b, i: (row(b, i), 1)),
                  pl.BlockSpec((1, tm, d), lambda b, i: (b, i, 0)),
                  pl.BlockSpec((1, 1, d), lambda b, i: (b, 0, 0)),
                  pl.BlockSpec((1, 1, 2, f), lambda b, i: (b, i, 0, 0)),
                  pl.BlockSpec(conv_w.shape, lambda b, i: (0, 0)),
                  pl.BlockSpec(wd_bf.shape, lambda b, i: (0, 0)),
                  pl.BlockSpec(gf_row.shape, lambda b, i: (0, 0))],
        out_specs=pl.BlockSpec((1, tm, d), lambda b, i: (b, i, 0)),
        out_shape=jax.ShapeDtypeStruct((bsz, seq, d), F32),
        compiler_params=_cparams(("parallel", "parallel")),
        name="ffn_long",
    )(uv, uv, x1, gt, halo_u, conv_w, wd_bf, gf_row)


def _ffn_short_kernel(u_ref, v_ref, x_ref, gt_ref, prev_ref, cw_ref, wd_ref, gf_ref, o_ref):
    u = u_ref[...]
    t_len = u.shape[0]
    ext = [prev_ref[0], prev_ref[1]] + [u[t] for t in range(t_len)]
    cw = cw_ref[...]
    uc = jnp.stack([cw[0:1] * ext[t] + cw[1:2] * ext[t + 1] + cw[2:3] * ext[t + 2] for t in range(t_len)])
    act = uc * _sigmoid(uc) * v_ref[...]
    o_ref[...] = _ffn_finish(act, x_ref[...], gt_ref[...], wd_ref, gf_ref)


def _ffn_short(uv3, x3, gt, prev, conv_w, wd_bf, gf_row):
    t_len, ns, d = x3.shape
    f = conv_w.shape[1]
    return pl.pallas_call(
        _ffn_short_kernel,
        grid=(1,),
        in_specs=[pl.BlockSpec((t_len, ns, f), lambda i: (0, 0, 0)),
                  pl.BlockSpec((t_len, ns, f), lambda i: (0, 0, 1)),
                  pl.BlockSpec((t_len, ns, d), lambda i: (0, 0, 0)),
                  pl.BlockSpec((1, ns, d), lambda i: (0, 0, 0)),
                  pl.BlockSpec((2, ns, f), lambda i: (0, 0, 0)),
                  pl.BlockSpec(conv_w.shape, lambda i: (0, 0)),
                  pl.BlockSpec(wd_bf.shape, lambda i: (0, 0)),
                  pl.BlockSpec(gf_row.shape, lambda i: (0, 0))],
        out_specs=pl.BlockSpec((t_len, ns, d), lambda i: (0, 0, 0)),
        out_shape=jax.ShapeDtypeStruct((t_len, ns, d), F32),
        compiler_params=_cparams(("arbitrary",)),
        name="ffn_short",
    )(uv3, uv3, x3, gt, prev, conv_w, wd_bf, gf_row)


def _pad_axis(a, axis, size):
    pad = [(0, 0)] * a.ndim
    pad[axis] = (0, size - a.shape[axis])
    return jnp.pad(a, pad)


def _split_weights(w_in):
    w_main = jnp.concatenate([w_in[:, 0:IN_OFF_KV],
                              w_in[:, IN_OFF_MG:IN_OFF_MG + 2 * D_MODEL],
                              w_in[:, IN_OFF_KV:IN_OFF_AG]], axis=1)
    src = np.zeros((AG_W,), np.int32)
    keep = np.zeros((AG_W,), np.float32)
    for g in range(N_KV):
        for c in range(3):
            for r in range(GROUP):
                src[g * LANES + c * GROUP + r] = IN_OFF_AG + c * N_KV * GROUP + g * GROUP + r
                keep[g * LANES + c * GROUP + r] = 1.0
    w_ag = w_in[:, src] * keep[None, :]
    return w_main.astype(BF16), w_ag.astype(BF16)


def kernel(x_prompt, x_sample, cache_cmp_k, cache_cmp_v, cache_slc_k, cache_slc_v, cache_win_k, cache_win_v,
           state_conv, state_ffn, page_table, c_prompt, c_sample, w_ada, b_ada, norm1_g, norm2_g, w_in, conv_w,
           cmp_pos_k, cmp_w1_k, cmp_b1_k, cmp_w2_k, cmp_pos_v, cmp_w1_v, cmp_b1_v, cmp_w2_v,
           w_out, w_up, ffn_conv_w, w_down, norm_f_g):
    depth = w_ada.shape[0]
    assert depth == 1, "single trunk layer"
    bsz, seq, d = x_prompt.shape
    ns, n_q, _ = x_sample.shape
    npages = page_table.shape[1]
    past = npages * PAGE
    wb = cache_win_k.shape[2]
    assert d == D_MODEL and seq % KEY_TILE == 0 and seq >= WIN_KEYS and n_q <= SUBLANES and past % SLC_BLK == 0

    w_main, w_ag = _split_weights(w_in[0])
    w_ada_bf = w_ada[0].astype(BF16)
    w_out_bf = w_out[0].astype(BF16)
    w_up_bf = w_up[0].astype(BF16)
    w_down_bf = w_down[0].astype(BF16)
    g1 = norm1_g[0].reshape(1, d)
    g2 = norm2_g[0].reshape(1, d)
    gf = norm_f_g.reshape(1, d)
    cw = conv_w[0]
    fcw = ffn_conv_w[0]
    wts_k = _compress_weights(cmp_pos_k[0], cmp_w1_k[0], cmp_b1_k[0], cmp_w2_k[0])
    wts_v = _compress_weights(cmp_pos_v[0], cmp_w1_v[0], cmp_b1_v[0], cmp_w2_v[0])

    n_c = bsz + ns
    n_c_pad = -(-n_c // SUBLANES) * SUBLANES
    c_all = _pad_axis(jnp.concatenate([c_prompt, c_sample], axis=0), 0, n_c_pad)
    mod = _adaln(c_all, w_ada_bf, b_ada[0].reshape(1, -1))
    mod_p = mod[:bsz].reshape(bsz, 6, 1, d)
    mod_s = mod[bsz:n_c].reshape(ns, 6, d)
    sh1_p, sc1_p, gt1_p, sh2_p, sc2_p, gt2_p = (mod_p[:, k] for k in range(6))
    sh1_s, sc1_s, gt1_s, sh2_s, sc2_s, gt2_s = (mod_s[:, k] for k in range(6))

    def kv_slices(z2d):
        return [z2d[:, COL_KV + k * KV_W:COL_KV + (k + 1) * KV_W] for k in range(6)]

    rows_p = bsz * seq
    tm_p = min(1024, seq)
    z_p, zag_p = _norm_mod_matmul(x_prompt.reshape(rows_p, d), g1, sc1_p, sh1_p, w_main, w_ag,
                                  tm=tm_p, tn=1280, rows_per_mod=seq, name="in_proj_prompt")
    kv_p = [a.reshape(bsz, seq, N_KV, HEAD_DIM) for a in kv_slices(z_p)]
    tab_p = jnp.arange(bsz * (seq // PAGE), dtype=jnp.int32)
    as_pages = lambda a: a.reshape(-1, PAGE, N_KV, HEAD_DIM).transpose(0, 2, 3, 1)[None]
    kc_p = _compress_paged(as_pages(kv_p[0]), tab_p, bsz, seq // PAGE, wts_k)
    vc_p = _compress_paged(as_pages(kv_p[1]), tab_p, bsz, seq // PAGE, wts_v)
    heads = [a.transpose(0, 2, 1, 3).astype(BF16) for a in kv_p[2:]]
    qh = z_p[:, COL_Q:COL_Q + d].reshape(bsz, seq, N_KV, GROUP, HEAD_DIM).transpose(0, 2, 3, 1, 4).astype(BF16)
    n_slc_p = seq // SLC_BLK
    n_lanes_p = -(-n_slc_p // LANES) * LANES
    nch_p = seq // CMP_STRIDE
    ovl_p = _overlap(nch_p, nch_p - 1, n_slc_p, n_lanes_p).T
    yb_h = _prompt_attention(qh, kc_p, vc_p, heads[0], heads[1], heads[2], heads[3],
                             zag_p.reshape(bsz, seq, AG_W), ovl_p, seq)
    yb_p = yb_h.transpose(0, 3, 1, 2, 4).reshape(bsz, seq, d)

    tm_m = min(512, seq)
    nt_m = seq // tm_m
    z4 = z_p.reshape(bsz, nt_m, tm_m, Z_W)
    conv0 = jnp.zeros((bsz, 1, 2, d), F32)
    halo_c = jnp.concatenate([conv0, z4[:, :-1, tm_m - 2:, COL_CG:COL_CG + d]], axis=1)
    halo_x = jnp.concatenate([jnp.ones((bsz, 1, 2, d), F32), z4[:, :-1, tm_m - 2:, COL_XIN:COL_XIN + d]], axis=1)
    x1_p, tail_p = _mixer_long(z_p, yb_p, x_prompt, gt1_p, halo_c, halo_x, cw, w_out_bf, tm=tm_m)
    conv_p = tail_p[:, -1]

    uv_p = _norm_mod_matmul(x1_p.reshape(rows_p, d), g2, sc2_p, sh2_p, w_up_bf, None,
                            tm=tm_p, tn=1408, rows_per_mod=seq, name="up_proj_prompt")
    tm_f = min(256, seq)
    nt_f = seq // tm_f
    u4 = uv_p.reshape(bsz, nt_f, tm_f, 2 * D_FF)
    halo_u = jnp.concatenate([jnp.zeros((bsz, 1, 2, D_FF), F32), u4[:, :-1, tm_f - 2:, :D_FF]], axis=1)
    y_prompt = _ffn_long(uv_p, x1_p, gt2_p, halo_u, fcw, w_down_bf, gf, tm=tm_f)
    ffn_p = u4[:, -1, tm_f - 2:, :D_FF]

    wbp = min(WINDOW, seq)
    outs_p = [a[None] for a in kv_p[:4]] + [kv_p[4][None, :, seq - wbp:], kv_p[5][None, :, seq - wbp:],
                                            conv_p[None], ffn_p[None]]

    rows_s = n_q * ns
    x_s3 = x_sample.transpose(1, 0, 2)
    tile_t = lambda a: jnp.tile(a, (n_q, 1))[None]
    z_s, zag_s = _norm_mod_matmul(x_s3.reshape(rows_s, d), g1, tile_t(sc1_s), tile_t(sh1_s), w_main, w_ag,
                                  tm=rows_s, tn=1280, rows_per_mod=rows_s, name="in_proj_sample")
    kv_s = [a.reshape(n_q, ns, N_KV, HEAD_DIM).transpose(1, 0, 2, 3) for a in kv_slices(z_s)]
    pool = lambda c: c.transpose(0, 1, 3, 4, 2)
    tab_s = page_table.reshape(-1).astype(jnp.int32)
    kc_s = _compress_paged(pool(cache_cmp_k), tab_s, ns, npages, wts_k)
    vc_s = _compress_paged(pool(cache_cmp_v), tab_s, ns, npages, wts_v)

    q_s = z_s[:, COL_Q:COL_Q + d].reshape(n_q, ns, N_KV, GROUP, HEAD_DIM)
    q1 = _pad_axis(q_s.transpose(1, 2, 3, 0, 4), 3, SUBLANES).reshape(ns, N_KV, GROUP * SUBLANES, HEAD_DIM)
    q2 = _pad_axis(q_s.transpose(1, 2, 0, 3, 4), 3, SUBLANES)
    n_slc_s = -(-(past + n_q) // SLC_BLK)
    n_lanes_s = -(-n_slc_s // LANES) * LANES
    nch_s = past // CMP_STRIDE
    ovl_s = _overlap(nch_s, nch_s - 1, n_slc_s, n_lanes_s)
    oc1, idx = _sample_cmp(q1.astype(BF16), kc_s, vc_s, ovl_s, past, n_slc_s)
    oc2 = _pad_axis(oc1.reshape(ns, N_KV, GROUP, SUBLANES, HEAD_DIM)[:, :, :, :n_q].transpose(0, 1, 3, 2, 4),
                    3, SUBLANES)

    pbk = past // SLC_BLK
    bpp = PAGE // SLC_BLK
    jp = jnp.minimum(idx[:, :, :n_q, :TOPK], pbk - 1)
    page = jnp.take_along_axis(page_table, (jp // bpp).reshape(ns, -1), axis=1).reshape(jp.shape)
    prow = (page * bpp + jp % bpp).reshape(-1).astype(jnp.int32)

    new_rows = [_pad_axis(a.transpose(0, 2, 1, 3), 2, SUBLANES) for a in kv_s[2:]]
    ag_s = zag_s.reshape(n_q, ns, N_KV, LANES)[..., :3 * GROUP].reshape(n_q, ns, N_KV, 3, GROUP)
    ag2 = _pad_axis(_pad_axis(ag_s.transpose(1, 2, 0, 4, 3), 3, SUBLANES), 4, LANES)
    ag2 = ag2.reshape(ns, N_KV, n_q * SUBLANES, LANES)
    y_h = _sample_selwin(prow, pool(cache_slc_k), pool(cache_slc_v), pool(cache_win_k), pool(cache_win_v),
                         idx, q2.astype(BF16), new_rows[0], new_rows[1], new_rows[2], new_rows[3], oc2, ag2, past)
    yb_s3 = y_h[:, :, :, :GROUP].transpose(2, 0, 1, 3, 4).reshape(n_q, ns, d)

    new_tail = lambda a: jnp.pad(a.transpose(0, 2, 3, 1), ((0, 0), (0, 0), (0, 0), (LANES - n_q, 0)))
    win_k_s = _win_shift(pool(cache_win_k), new_tail(kv_s[4]), n_q).transpose(0, 1, 4, 2, 3)
    win_v_s = _win_shift(pool(cache_win_v), new_tail(kv_s[5]), n_q).transpose(0, 1, 4, 2, 3)

    z_s3 = z_s.reshape(n_q, ns, Z_W)
    x1_s3, conv_s3 = _mixer_short(z_s3, yb_s3, x_s3, gt1_s[None], state_conv[0].transpose(1, 0, 2), cw, w_out_bf)
    uv_s = _norm_mod_matmul(x1_s3.reshape(rows_s, d), g2, tile_t(sc2_s), tile_t(sh2_s), w_up_bf, None,
                            tm=rows_s, tn=1408, rows_per_mod=rows_s, name="up_proj_sample")
    uv_s3 = uv_s.reshape(n_q, ns, 2 * D_FF)
    y_s3 = _ffn_short(uv_s3, x1_s3, gt2_s[None], state_ffn[0].transpose(1, 0, 2), fcw, w_down_bf, gf)
    y_sample = y_s3.transpose(1, 0, 2)
    ffn_ext = jnp.concatenate([state_ffn[0], uv_s3[:, :, :D_FF].transpose(1, 0, 2)], axis=1)
    ffn_s = ffn_ext[:, n_q:]
    conv_s = conv_s3.transpose(1, 0, 2)

    outs_s = [a[None] for a in kv_s[:4]] + [win_k_s, win_v_s, conv_s[None], ffn_s[None]]
    return (y_prompt, y_sample, *outs_p, *outs_s)
```

```python
import functools

import numpy as np
import jax
import jax.numpy as jnp
from jax import lax
from jax.experimental import pallas as pl
from jax.experimental.pallas import tpu as pltpu

F32 = jnp.float32
BF16 = jnp.bfloat16

D_MODEL = 1024
N_KV = 4
GROUP = 4
HEAD_DIM = 64
KV_W = N_KV * HEAD_DIM
GPAIRS = N_KV // 2
CMP_BLK = 32
CMP_STRIDE = 16
CMP_HID = 128
SLC_BLK = 64
TOPK = 16
WINDOW = 512
PAGE = 128
D_FF = 2816
EPS = 1e-6
NEG = -1e30
BIG = 1e30
SCALE = HEAD_DIM ** -0.5

LANES = 128
SUBLANES = 8
VMEM_LIMIT = 56 * 1024 * 1024

COL_BG, COL_CG, COL_XIN, COL_Q, COL_MG, COL_KV = 0, 1024, 2048, 3072, 4096, 6144
Z_W = 7680
AG_W = N_KV * LANES
IN_OFF_Q = 3072
IN_OFF_KV = 4096
IN_OFF_AG = 5632
IN_OFF_MG = 5680

KEY_TILE = 512
WIN_KEYS = WINDOW + 128
Q_BLK = 128


def _cparams(sem, vmem=VMEM_LIMIT):
    return pltpu.CompilerParams(dimension_semantics=sem, vmem_limit_bytes=vmem)


def _dot(a, b):
    return jnp.dot(a, b, preferred_element_type=F32)


def _dot_nt(a, b):
    return lax.dot_general(a, b, (((1,), (1,)), ((), ())), preferred_element_type=F32)


def _sigmoid(x):
    return 1.0 / (1.0 + jnp.exp(-x))


def _softmax_parts(s, mask):
    s = jnp.where(mask, s, NEG)
    m = jnp.max(s, axis=-1, keepdims=True)
    return s, m


def _topk_select(score, pos, k, axis, want_idx=False):
    n = score.shape[axis]
    sel = jnp.zeros(score.shape, F32)
    idx_mat = jnp.zeros((score.shape[0], LANES), F32) if want_idx else None
    out_lane = lax.broadcasted_iota(jnp.int32, (score.shape[0], LANES), 1) if want_idx else None
    for i in range(k):
        m = jnp.max(score, axis=axis, keepdims=True)
        cand = jnp.where(score == m, pos, float(n))
        idx = jnp.min(cand, axis=axis, keepdims=True)
        hit = pos == idx
        sel = jnp.where(hit, 1.0, sel)
        score = jnp.where(hit, -jnp.inf, score)
        if want_idx:
            idx_mat = jnp.where(out_lane == i, idx, idx_mat)
    return sel, idx_mat


def _adaln_kernel(c_ref, w_ref, b_ref, o_ref):
    o_ref[...] = _dot(c_ref[...].astype(BF16), w_ref[...]) + b_ref[...]


def _adaln(c_all, w_bf, b_row):
    rows, d = c_all.shape
    n = w_bf.shape[1]
    tn = 1536
    return pl.pallas_call(
        _adaln_kernel,
        grid=(n // tn,),
        in_specs=[pl.BlockSpec((rows, d), lambda j: (0, 0)),
                  pl.BlockSpec((d, tn), lambda j: (0, j)),
                  pl.BlockSpec((1, tn), lambda j: (0, j))],
        out_specs=pl.BlockSpec((rows, tn), lambda j: (0, j)),
        out_shape=jax.ShapeDtypeStruct((rows, n), F32),
        compiler_params=_cparams(("parallel",)),
        name="adaln",
    )(c_all, w_bf, b_row)


def _norm_mod_mm_kernel(*refs, has_gate):
    if has_gate:
        x_ref, g_ref, sc_ref, sh_ref, w_ref, wg_ref, o_ref, og_ref, h_ref = refs
    else:
        x_ref, g_ref, sc_ref, sh_ref, w_ref, o_ref, h_ref = refs

    @pl.when(pl.program_id(1) == 0)
    def _():
        x = x_ref[...]
        ms = jnp.mean(x * x, axis=-1, keepdims=True)
        y = x * lax.rsqrt(ms + EPS) * g_ref[...]
        h = (y * (1.0 + sc_ref[0]) + sh_ref[0]).astype(BF16)
        h_ref[...] = h
        if has_gate:
            og_ref[...] = _dot(h, wg_ref[...])

    o_ref[...] = _dot(h_ref[...], w_ref[...])


def _norm_mod_matmul(x2d, g_row, sc, sh, w_bf, wg_bf, *, tm, tn, rows_per_mod, name):
    rows, d = x2d.shape
    n = w_bf.shape[1]
    per_row = sc.shape[1] != 1
    if per_row:
        mod_spec = pl.BlockSpec((1, tm, d), lambda i, j: (0, i, 0))
    else:
        mod_spec = pl.BlockSpec((1, 1, d), lambda i, j: ((i * tm) // rows_per_mod, 0, 0))
    has_gate = wg_bf is not None
    in_specs = [pl.BlockSpec((tm, d), lambda i, j: (i, 0)),
                pl.BlockSpec((1, d), lambda i, j: (0, 0)),
                mod_spec, mod_spec,
                pl.BlockSpec((d, tn), lambda i, j: (0, j))]
    out_specs = [pl.BlockSpec((tm, tn), lambda i, j: (i, j))]
    out_shape = [jax.ShapeDtypeStruct((rows, n), F32)]
    args = [x2d, g_row, sc, sh, w_bf]
    if has_gate:
        ng = wg_bf.shape[1]
        in_specs.append(pl.BlockSpec((d, ng), lambda i, j: (0, 0)))
        out_specs.append(pl.BlockSpec((tm, ng), lambda i, j: (i, 0)))
        out_shape.append(jax.ShapeDtypeStruct((rows, ng), F32))
        args.append(wg_bf)
    res = pl.pallas_call(
        functools.partial(_norm_mod_mm_kernel, has_gate=has_gate),
        grid=(rows // tm, n // tn),
        in_specs=in_specs,
        out_specs=out_specs,
        out_shape=out_shape,
        scratch_shapes=[pltpu.VMEM((tm, d), BF16)],
        compiler_params=_cparams(("parallel", "arbitrary")),
        name=name,
    )(*args)
    return res if has_gate else res[0]


def _compress_kernel(tab_ref, pool_ref, wp_ref, pos_ref, w1_ref, b1_ref, w2_ref, o_ref, xt_ref, x_ref, sem,
                     *, npages):
    step = pl.program_id(0) * GPAIRS + pl.program_id(1)
    nsteps = pl.num_programs(0) * GPAIRS
    slot = lax.rem(step, 2)
    nch = npages * (PAGE // CMP_STRIDE)
    ntok = npages * PAGE

    def page_copy(st, sl, p):
        page = tab_ref[lax.div(st, GPAIRS) * npages + p]
        return pltpu.make_async_copy(pool_ref.at[0, page, lax.rem(st, GPAIRS)],
                                     xt_ref.at[sl, :, pl.ds(pl.multiple_of(p * PAGE, PAGE), PAGE)],
                                     sem.at[sl])

    def start_all(st, sl):
        def body(p, carry):
            page_copy(st, sl, p).start()
            return carry
        lax.fori_loop(0, npages, body, 0)

    def wait_all(st, sl):
        def body(p, carry):
            page_copy(st, sl, p).wait()
            return carry
        lax.fori_loop(0, npages, body, 0)

    @pl.when(step == 0)
    def _():
        start_all(0, 0)

    @pl.when(step + 1 < nsteps)
    def _():
        start_all(step + 1, 1 - slot)

    wait_all(step, slot)

    slab = min(ntok, 512)
    for j in range(ntok // slab):
        x_ref[pl.ds(j * slab, slab), :] = xt_ref[slot, :, pl.ds(j * slab, slab)].T

    const = _dot(pos_ref[...], w1_ref[...])[0:1, :] + b1_ref[...]
    low = lax.broadcasted_iota(jnp.int32, (nch, 2 * HEAD_DIM), 1) < HEAD_DIM
    accs = [None, None]
    for s2 in range(CMP_STRIDE // 2):
        a0 = x_ref[pl.ds(2 * s2, nch, stride=CMP_STRIDE), :]
        a1 = x_ref[pl.ds(2 * s2 + 1, nch, stride=CMP_STRIDE), :]
        pairs = (jnp.where(low, a0, pltpu.roll(a1, HEAD_DIM, 1)),
                 jnp.where(low, pltpu.roll(a0, HEAD_DIM, 1), a1))
        for gi in range(2):
            part = _dot(pairs[gi].astype(BF16), wp_ref[s2])
            accs[gi] = part if accs[gi] is None else accs[gi] + part
    rows = lax.broadcasted_iota(jnp.int32, (nch, HEAD_DIM), 0)
    for gi in range(2):
        first = accs[gi][:, :CMP_HID]
        second = accs[gi][:, CMP_HID:]
        hid = first + pltpu.roll(second, nch - 1, 0) + const
        act = hid * _sigmoid(hid)
        out = _dot(act.astype(BF16), w2_ref[...])
        o_ref[0, gi] = jnp.where(rows < nch - 1, out, 0.0)


def _compress_paged(pool_t, table_flat, nseq, npages, wts):
    wp, posb, w1b, b1r, w2b = wts
    nch = npages * (PAGE // CMP_STRIDE)
    ntok = npages * PAGE
    pool_pairs = pool_t.reshape(1, -1, GPAIRS, 2 * HEAD_DIM, PAGE)
    grid_spec = pltpu.PrefetchScalarGridSpec(
        num_scalar_prefetch=1,
        grid=(nseq, GPAIRS),
        in_specs=[pl.BlockSpec(memory_space=pl.ANY),
                  pl.BlockSpec(wp.shape, lambda b, g, t: (0, 0, 0)),
                  pl.BlockSpec(posb.shape, lambda b, g, t: (0, 0)),
                  pl.BlockSpec(w1b.shape, lambda b, g, t: (0, 0)),
                  pl.BlockSpec(b1r.shape, lambda b, g, t: (0, 0)),
                  pl.BlockSpec(w2b.shape, lambda b, g, t: (0, 0))],
        out_specs=pl.BlockSpec((1, 2, nch, HEAD_DIM), lambda b, g, t: (b, g, 0, 0)),
        scratch_shapes=[pltpu.VMEM((2, 2 * HEAD_DIM, ntok), F32),
                        pltpu.VMEM((ntok, 2 * HEAD_DIM), F32),
                        pltpu.SemaphoreType.DMA((2,))],
    )
    return pl.pallas_call(
        functools.partial(_compress_kernel, npages=npages),
        grid_spec=grid_spec,
        out_shape=jax.ShapeDtypeStruct((nseq, N_KV, nch, HEAD_DIM), F32),
        compiler_params=_cparams(("arbitrary", "arbitrary")),
        name="compress",
    )(table_flat, pool_pairs, wp, posb, w1b, b1r, w2b)


def _compress_weights(pos, w1, b1, w2):
    w1r = w1.reshape(2, CMP_STRIDE, HEAD_DIM, CMP_HID)
    cat = jnp.concatenate([w1r[0], w1r[1]], axis=-1)
    wp = cat.reshape(CMP_STRIDE // 2, 2 * HEAD_DIM, 2 * CMP_HID)
    posb = jnp.broadcast_to(pos.reshape(1, -1), (SUBLANES, CMP_BLK * HEAD_DIM))
    return (wp.astype(BF16), posb.astype(BF16), w1.astype(BF16), b1.reshape(1, -1), w2.astype(BF16))


def _col_softmax_parts(s, mask, col_live):
    s = jnp.where(mask, s, NEG)
    e = jnp.exp(s - jnp.max(s, axis=0, keepdims=True))
    scale = jnp.where(col_live, 1.0 / jnp.maximum(jnp.sum(e, axis=0, keepdims=True), 1e-30), 0.0)
    return e, scale


def _prompt_attn_kernel(q_ref, kc_ref, vct_ref, ks_ref, vst_ref, kw_ref, vwt_ref, ag_ref, ovl_ref, o_ref,
                        bias_ref, *, seq_len, n_slc, n_lanes):
    i = pl.program_id(2)
    s0 = i * Q_BLK
    cols = GROUP * Q_BLK
    q = (q_ref[0, 0].reshape(cols, HEAD_DIM).astype(F32) * SCALE).astype(BF16)

    def qpos_of(shape):
        return s0 + (lax.broadcasted_iota(jnp.int32, shape, 1) & (Q_BLK - 1))

    nch = kc_ref.shape[2]
    s = _dot_nt(kc_ref[0, 0].astype(BF16), q)
    blk_end = lax.broadcasted_iota(jnp.int32, (nch, cols), 0) * CMP_STRIDE + (CMP_BLK - 1)
    e, scale = _col_softmax_parts(s, blk_end <= qpos_of((nch, cols)), qpos_of((1, cols)) >= CMP_BLK - 1)
    o_cmp = _dot(vct_ref[0, 0].astype(BF16), e.astype(BF16)) * scale

    psum = e[:, 0:Q_BLK] * scale[:, 0:Q_BLK]
    for r in range(1, GROUP):
        psum = psum + e[:, r * Q_BLK:(r + 1) * Q_BLK] * scale[:, r * Q_BLK:(r + 1) * Q_BLK]
    p_hi = psum.astype(BF16)
    p_lo = (psum - p_hi.astype(F32)).astype(BF16)
    ovl_t = ovl_ref[...]
    imp_t = _dot(ovl_t, p_hi) + _dot(ovl_t, p_lo)
    j = lax.broadcasted_iota(jnp.int32, (n_lanes, Q_BLK), 0)
    qp = s0 + lax.broadcasted_iota(jnp.int32, (n_lanes, Q_BLK), 1)
    tb = lax.shift_right_logical(qp, 6)
    forced = (j == 0) | (j == tb) | (j == tb - 1)
    score = jnp.where(forced, BIG, jnp.where(j * SLC_BLK <= qp, imp_t, NEG))
    score = jnp.where(j < n_slc, score, -jnp.inf)
    sel_t, _ = _topk_select(score, j.astype(F32), min(TOPK, n_slc), 0)
    bias = jnp.where(sel_t > 0.5, 0.0, NEG)
    bias_ref[...] = jnp.concatenate([bias] * GROUP, axis=1)

    blocks_per_tile = KEY_TILE // SLC_BLK

    def scores(t):
        k0 = pl.multiple_of(t * KEY_TILE, KEY_TILE)
        return _dot_nt(ks_ref[0, 0, pl.ds(k0, KEY_TILE), :], q)

    def update(t, sc, state, causal):
        m_old, l_old, acc = state
        k0 = pl.multiple_of(t * KEY_TILE, KEY_TILE)
        bt = bias_ref[pl.ds(pl.multiple_of(t * blocks_per_tile, blocks_per_tile), blocks_per_tile), :]
        sc = (sc.reshape(blocks_per_tile, SLC_BLK, cols) + bt[:, None, :]).reshape(KEY_TILE, cols)
        if causal:
            kpos = k0 + lax.broadcasted_iota(jnp.int32, (KEY_TILE, cols), 0)
            sc = jnp.where(kpos <= qpos_of((KEY_TILE, cols)), sc, NEG)
        m_new = jnp.maximum(m_old, jnp.max(sc, axis=0, keepdims=True))
        pt = jnp.exp(sc - m_new)
        alpha = jnp.exp(m_old - m_new)
        l_new = alpha * l_old + jnp.sum(pt, axis=0, keepdims=True)
        acc = alpha * acc + _dot(vst_ref[0, 0, :, pl.ds(k0, KEY_TILE)], pt.astype(BF16))
        return m_new, l_new, acc

    def full_tile(t, carry):
        nxt = scores(t + 1)
        state = update(t, carry[0], carry[1:], False)
        return (nxt,) + state

    n_full = s0 // KEY_TILE
    init = (scores(0), jnp.full((1, cols), NEG, F32), jnp.zeros((1, cols), F32),
            jnp.zeros((HEAD_DIM, cols), F32))
    carry = lax.fori_loop(0, n_full, full_tile, init)
    _, l_sel, acc_sel = update(n_full, carry[0], carry[1:], True)
    o_sel = acc_sel / jnp.maximum(l_sel, 1e-30)

    w0 = pl.multiple_of(jnp.clip(s0 - WINDOW, 0, seq_len - WIN_KEYS), Q_BLK)
    sw = _dot_nt(kw_ref[0, 0, pl.ds(w0, WIN_KEYS), :], q)
    kpos = w0 + lax.broadcasted_iota(jnp.int32, (WIN_KEYS, cols), 0)
    qpw = qpos_of((WIN_KEYS, cols))
    wmask = jnp.where(kpos <= qpw, jnp.where(kpos > qpw - WINDOW, 1.0, 0.0), 0.0) > 0.5
    ew, wscale = _col_softmax_parts(sw, wmask, True)
    o_win = _dot(vwt_ref[0, 0, :, pl.ds(w0, WIN_KEYS)], ew.astype(BF16)) * wscale

    gate = _sigmoid(ag_ref[0]).T
    for r in range(GROUP):
        cs = slice(r * Q_BLK, (r + 1) * Q_BLK)
        y = (gate[r:r + 1] * o_cmp[:, cs]
             + gate[GROUP + r:GROUP + r + 1] * o_sel[:, cs]
             + gate[2 * GROUP + r:2 * GROUP + r + 1] * o_win[:, cs])
        o_ref[0, 0, r] = y


def _prompt_attention(qh, kc, vct, ks, vst, kw, vwt, zag, ovl, seq_len):
    bsz = qh.shape[0]
    nch = kc.shape[2]
    n_slc = seq_len // SLC_BLK
    n_lanes = ovl.shape[0]
    cols = GROUP * Q_BLK
    k_spec = pl.BlockSpec((1, 1, seq_len, HEAD_DIM), lambda b, g, i: (b, g, 0, 0))
    vt_spec = pl.BlockSpec((1, 1, HEAD_DIM, seq_len), lambda b, g, i: (b, g, 0, 0))
    return pl.pallas_call(
        functools.partial(_prompt_attn_kernel, seq_len=seq_len, n_slc=n_slc, n_lanes=n_lanes),
        grid=(bsz, N_KV, seq_len // Q_BLK),
        in_specs=[pl.BlockSpec((1, 1, GROUP, Q_BLK, HEAD_DIM), lambda b, g, i: (b, g, 0, i, 0)),
                  pl.BlockSpec((1, 1, nch, HEAD_DIM), lambda b, g, i: (b, g, 0, 0)),
                  pl.BlockSpec((1, 1, HEAD_DIM, nch), lambda b, g, i: (b, g, 0, 0)),
                  k_spec, vt_spec, k_spec, vt_spec,
                  pl.BlockSpec((1, Q_BLK, LANES), lambda b, g, i: (b, i, g)),
                  pl.BlockSpec(ovl.shape, lambda b, g, i: (0, 0))],
        out_specs=pl.BlockSpec((1, 1, GROUP, HEAD_DIM, Q_BLK), lambda b, g, i: (b, g, 0, 0, i)),
        out_shape=jax.ShapeDtypeStruct((bsz, N_KV, GROUP, HEAD_DIM, seq_len), F32),
        scratch_shapes=[pltpu.VMEM((n_lanes, cols), F32)],
        compiler_params=_cparams(("parallel", "parallel", "arbitrary")),
        name="prompt_attn",
    )(qh, kc, vct, ks, vst, kw, vwt, zag, ovl)


def _overlap(n_rows, n_cmp, n_slc, n_lanes):
    i = np.arange(n_rows)[:, None] * CMP_STRIDE
    jj = np.arange(n_lanes)[None, :] * SLC_BLK
    ovl = (i <= jj + SLC_BLK - 1) & (i + CMP_BLK - 1 >= jj)
    ovl &= (np.arange(n_rows)[:, None] < n_cmp) & (np.arange(n_lanes)[None, :] < n_slc)
    return jnp.asarray(ovl.astype(np.float32), dtype=BF16)


def _sample_cmp_kernel(q_ref, kc_ref, vc_ref, ovl_ref, oc_ref, idx_ref, *, past, n_slc, n_lanes):
    nch = kc_ref.shape[2]
    rows = GROUP * SUBLANES
    ovl = ovl_ref[...]
    imps = []
    for g in range(N_KV):
        q = (q_ref[0, g].astype(F32) * SCALE).astype(BF16)
        kc = kc_ref[0, g].astype(BF16)
        vc = vc_ref[0, g].astype(BF16)
        s = _dot_nt(q, kc)
        qpos = past + (lax.broadcasted_iota(jnp.int32, (rows, nch), 0) & (SUBLANES - 1))
        blk_end = lax.broadcasted_iota(jnp.int32, (rows, nch), 1) * CMP_STRIDE + (CMP_BLK - 1)
        mask = blk_end <= qpos
        s, m = _softmax_parts(s, mask)
        e = jnp.where(mask, jnp.exp(s - m), 0.0)
        p = e / jnp.maximum(jnp.sum(e, axis=-1, keepdims=True), 1e-30)
        oc_ref[0, g] = _dot(p.astype(BF16), vc)
        psum = p[0:SUBLANES]
        for r in range(1, GROUP):
            psum = psum + p[r * SUBLANES:(r + 1) * SUBLANES]
        p_hi = psum.astype(BF16)
        p_lo = (psum - p_hi.astype(F32)).astype(BF16)
        imps.append(_dot(p_hi, ovl) + _dot(p_lo, ovl))
    imp = jnp.concatenate(imps, axis=0)
    sel_rows = N_KV * SUBLANES
    j = lax.broadcasted_iota(jnp.int32, (sel_rows, n_lanes), 1)
    qp = past + (lax.broadcasted_iota(jnp.int32, (sel_rows, n_lanes), 0) & (SUBLANES - 1))
    tb = lax.shift_right_logical(qp, 6)
    forced = (j == 0) | (j == tb) | (j == tb - 1)
    score = jnp.where(forced, BIG, jnp.where(j * SLC_BLK <= qp, imp, NEG))
    score = jnp.where(j < n_slc, score, -jnp.inf)
    _, idx = _topk_select(score, j.astype(F32), TOPK, 1, True)
    idx_ref[0] = idx.astype(jnp.int32).reshape(N_KV, SUBLANES, LANES)


def _sample_cmp(q1, kc, vc, ovl, past, n_slc):
    ns = q1.shape[0]
    nch = kc.shape[2]
    rows = GROUP * SUBLANES
    return pl.pallas_call(
        functools.partial(_sample_cmp_kernel, past=past, n_slc=n_slc, n_lanes=ovl.shape[1]),
        grid=(ns,),
        in_specs=[pl.BlockSpec((1, N_KV, rows, HEAD_DIM), lambda b: (b, 0, 0, 0)),
                  pl.BlockSpec((1, N_KV, nch, HEAD_DIM), lambda b: (b, 0, 0, 0)),
                  pl.BlockSpec((1, N_KV, nch, HEAD_DIM), lambda b: (b, 0, 0, 0)),
                  pl.BlockSpec(ovl.shape, lambda b: (0, 0))],
        out_specs=[pl.BlockSpec((1, N_KV, rows, HEAD_DIM), lambda b: (b, 0, 0, 0)),
                   pl.BlockSpec((1, N_KV, SUBLANES, LANES), lambda b: (b, 0, 0, 0))],
        out_shape=[jax.ShapeDtypeStruct((ns, N_KV, rows, HEAD_DIM), F32),
                   jax.ShapeDtypeStruct((ns, N_KV, SUBLANES, LANES), jnp.int32)],
        compiler_params=_cparams(("parallel",)),
        name="sample_cmp",
    )(q1, kc, vc, ovl)


def _sample_selwin_kernel(prow_ref, sk_ref, sv_ref, wk_ref, wv_ref, idx_ref, q_ref, ksn_ref, vsn_ref,
                          kwn_ref, vwn_ref, oc_ref, ag_ref, o_ref, kbuf, vbuf, sem,
                          *, n_q, past, wb):
    b = pl.program_id(0)
    g = pl.program_id(1)
    step = b * N_KV + g
    nsteps = pl.num_programs(0) * N_KV
    slot = lax.rem(step, 2)
    pb = past // SLC_BLK
    nkeys = TOPK * PAGE

    def copies(st, sl):
        sg = lax.rem(st, N_KV)
        res = []
        for qi in range(n_q):
            for k in range(TOPK):
                page = lax.shift_right_logical(prow_ref[(st * n_q + qi) * TOPK + k], 1)
                dst = pl.ds(k * PAGE, PAGE)
                res.append(pltpu.make_async_copy(sk_ref.at[0, page, sg], kbuf.at[sl, qi, :, dst], sem.at[sl]))
                res.append(pltpu.make_async_copy(sv_ref.at[0, page, sg], vbuf.at[sl, qi, :, dst], sem.at[sl]))
        return res

    @pl.when(step == 0)
    def _():
        for cp in copies(0, 0):
            cp.start()

    @pl.when(step + 1 < nsteps)
    def _():
        for cp in copies(step + 1, 1 - slot):
            cp.start()

    for cp in copies(step, slot):
        cp.wait()

    rows = n_q * SUBLANES
    gate = _sigmoid(ag_ref[0, 0])
    ksn = ksn_ref[0, 0].astype(BF16)
    vsn = vsn_ref[0, 0].astype(BF16)
    new_t = lax.broadcasted_iota(jnp.int32, (SUBLANES, SUBLANES), 1)
    lane_k = lax.broadcasted_iota(jnp.int32, (SUBLANES, LANES), 1)
    slot_row = lax.broadcasted_iota(jnp.int32, (LANES, nkeys), 0)
    key_col = lax.broadcasted_iota(jnp.int32, (LANES, nkeys), 1)
    same_slot = slot_row == lax.shift_right_logical(key_col, 7)
    upper = (lax.shift_right_logical(key_col, 6) & 1) == 1
    expand_lo = jnp.where(same_slot, jnp.where(upper, 0.0, 1.0), 0.0).astype(BF16)
    expand_hi = jnp.where(same_slot, jnp.where(upper, 1.0, 0.0), 0.0).astype(BF16)
    idx_all = idx_ref[0, 0]

    o_sel_rows = []
    for qi in range(n_q):
        qq = (q_ref[0, 0, qi].astype(F32) * SCALE).astype(BF16)
        kg = kbuf[slot, qi].astype(BF16)
        vg = vbuf[slot, qi].astype(BF16)
        idx_row = jnp.broadcast_to(idx_all[qi:qi + 1, :], (SUBLANES, LANES))
        valid_k = jnp.where(lane_k < TOPK, 1.0, 0.0)
        in_past = jnp.where(idx_row < pb, valid_k, 0.0)
        odd = (jnp.minimum(idx_row, pb - 1) & 1) == 1
        is_new = jnp.where(idx_row == pb, valid_k, 0.0)
        has_new = jnp.max(is_new, axis=-1, keepdims=True)
        sel_lo = jnp.where(odd, 0.0, in_past).astype(BF16)
        sel_hi = jnp.where(odd, in_past, 0.0).astype(BF16)
        m1 = (_dot(sel_lo, expand_lo) + _dot(sel_hi, expand_hi)) > 0.5
        s1 = jnp.where(m1, _dot(qq, kg), NEG)
        m2 = jnp.where(new_t <= qi, jnp.where(new_t < n_q, has_new, 0.0), 0.0) > 0.5
        s2 = jnp.where(m2, _dot_nt(qq, ksn), NEG)
        mx = jnp.maximum(jnp.max(s1, axis=-1, keepdims=True), jnp.max(s2, axis=-1, keepdims=True))
        e1 = jnp.where(m1, jnp.exp(s1 - mx), 0.0)
        e2 = jnp.where(m2, jnp.exp(s2 - mx), 0.0)
        den = jnp.maximum(jnp.sum(e1, axis=-1, keepdims=True) + jnp.sum(e2, axis=-1, keepdims=True), 1e-30)
        o = _dot_nt((e1 / den).astype(BF16), vg) + _dot((e2 / den).astype(BF16), vsn)
        o_sel_rows.append(o)
    o_sel = jnp.concatenate(o_sel_rows, axis=0)

    qa = (q_ref[0, 0].reshape(rows, HEAD_DIM).astype(F32) * SCALE).astype(BF16)
    kwc = wk_ref[0, 0, 0].astype(BF16)
    vwc = wv_ref[0, 0, 0].astype(BF16)
    kwn = kwn_ref[0, 0].astype(BF16)
    vwn = vwn_ref[0, 0].astype(BF16)
    qpos = past + lax.shift_right_logical(lax.broadcasted_iota(jnp.int32, (rows, wb), 0), 3)
    kpos = (past - wb) + lax.broadcasted_iota(jnp.int32, (rows, wb), 1)
    mw1 = jnp.where(kpos <= qpos, jnp.where(kpos > qpos - WINDOW, jnp.where(kpos >= 0, 1.0, 0.0), 0.0), 0.0) > 0.5
    sw1 = jnp.where(mw1, _dot(qa, kwc), NEG)
    qi_n = lax.shift_right_logical(lax.broadcasted_iota(jnp.int32, (rows, SUBLANES), 0), 3)
    t_n = lax.broadcasted_iota(jnp.int32, (rows, SUBLANES), 1)
    mw2 = jnp.where(t_n <= qi_n, jnp.where(t_n < n_q, jnp.where(t_n > qi_n - WINDOW, 1.0, 0.0), 0.0), 0.0) > 0.5
    sw2 = jnp.where(mw2, _dot_nt(qa, kwn), NEG)
    mx = jnp.maximum(jnp.max(sw1, axis=-1, keepdims=True), jnp.max(sw2, axis=-1, keepdims=True))
    e1 = jnp.where(mw1, jnp.exp(sw1 - mx), 0.0)
    e2 = jnp.where(mw2, jnp.exp(sw2 - mx), 0.0)
    den = jnp.maximum(jnp.sum(e1, axis=-1, keepdims=True) + jnp.sum(e2, axis=-1, keepdims=True), 1e-30)
    o_win = _dot_nt((e1 / den).astype(BF16), vwc) + _dot((e2 / den).astype(BF16), vwn)

    o_cmp = oc_ref[0, 0].reshape(rows, HEAD_DIM)
    y = gate[:, 0:1] * o_cmp + gate[:, 1:2] * o_sel + gate[:, 2:3] * o_win
    o_ref[0, 0] = y.reshape(n_q, SUBLANES, HEAD_DIM)


def _sample_selwin(prow, slc_k, slc_v, win_k, win_v, idx, q2, ksn, vsn, kwn, vwn, oc2, ag2, past):
    ns, _, n_q = q2.shape[:3]
    wb = win_k.shape[-1]
    rows = n_q * SUBLANES
    nkeys = TOPK * PAGE
    any_spec = pl.BlockSpec(memory_space=pl.ANY)
    win_spec = pl.BlockSpec((1, 1, 1, HEAD_DIM, wb), lambda b, g, pr: (0, b, g, 0, 0))
    new_spec = pl.BlockSpec((1, 1, SUBLANES, HEAD_DIM), lambda b, g, pr: (b, g, 0, 0))
    qspec = pl.BlockSpec((1, 1, n_q, SUBLANES, HEAD_DIM), lambda b, g, pr: (b, g, 0, 0, 0))
    grid_spec = pltpu.PrefetchScalarGridSpec(
        num_scalar_prefetch=1,
        grid=(ns, N_KV),
        in_specs=[any_spec, any_spec, win_spec, win_spec,
                  pl.BlockSpec((1, 1, SUBLANES, LANES), lambda b, g, pr: (b, g, 0, 0)),
                  qspec, new_spec, new_spec, new_spec, new_spec, qspec,
                  pl.BlockSpec((1, 1, rows, LANES), lambda b, g, pr: (b, g, 0, 0))],
        out_specs=qspec,
        scratch_shapes=[pltpu.VMEM((2, n_q, HEAD_DIM, nkeys), F32),
                        pltpu.VMEM((2, n_q, HEAD_DIM, nkeys), F32),
                        pltpu.SemaphoreType.DMA((2,))],
    )
    return pl.pallas_call(
        functools.partial(_sample_selwin_kernel, n_q=n_q, past=past, wb=wb),
        grid_spec=grid_spec,
        out_shape=jax.ShapeDtypeStruct((ns, N_KV, n_q, SUBLANES, HEAD_DIM), F32),
        compiler_params=_cparams(("arbitrary", "arbitrary")),
        name="sample_selwin",
    )(prow, slc_k, slc_v, win_k, win_v, idx, q2, ksn, vsn, kwn, vwn, oc2, ag2)


def _win_shift_kernel(w_ref, n_ref, o_ref, *, n_new):
    wb = w_ref.shape[-1]
    lane = lax.broadcasted_iota(jnp.int32, (HEAD_DIM, LANES), 1)
    for g in range(N_KV):
        shifted = pltpu.roll(w_ref[0, 0, g], wb - n_new, 1)
        o_ref[0, 0, g] = shifted
        o_ref[0, 0, g, :, wb - LANES:wb] = jnp.where(lane >= LANES - n_new, n_ref[0, g], shifted[:, wb - LANES:wb])


def _win_shift(win_t, new_t, n_new):
    ns, wb = win_t.shape[1], win_t.shape[-1]
    return pl.pallas_call(
        functools.partial(_win_shift_kernel, n_new=n_new),
        grid=(ns,),
        in_specs=[pl.BlockSpec((1, 1, N_KV, HEAD_DIM, wb), lambda b: (0, b, 0, 0, 0)),
                  pl.BlockSpec((1, N_KV, HEAD_DIM, LANES), lambda b: (b, 0, 0, 0))],
        out_specs=pl.BlockSpec((1, 1, N_KV, HEAD_DIM, wb), lambda b: (0, b, 0, 0, 0)),
        out_shape=jax.ShapeDtypeStruct(win_t.shape, F32),
        compiler_params=_cparams(("parallel",)),
        name="win_shift",
    )(win_t, new_t)


def _merge_project(bg, conv, mg, yb, x, gt, wout_ref):
    ya = bg * conv
    c = ya.shape[-1]
    mix = _sigmoid(mg[..., :c]) * ya + _sigmoid(mg[..., c:]) * yb
    shp = mix.shape
    proj = _dot(mix.reshape(-1, c).astype(BF16), wout_ref[...]).reshape(shp[:-1] + (wout_ref.shape[1],))
    return x + gt * proj


def _mixer_long_kernel(bg_ref, cg_ref, xin_ref, mg_ref, yb_ref, x_ref, gt_ref, hc_ref, hx_ref, cw_ref,
                       wout_ref, o_ref, tail_ref):
    u = cg_ref[...] * xin_ref[...]
    tm = u.shape[0]
    up = hc_ref[0, 0] * hx_ref[0, 0]
    row = lax.broadcasted_iota(jnp.int32, u.shape, 0)
    u1 = jnp.where(row == 0, up[1:2], pltpu.roll(u, 1, 0))
    u2 = jnp.where(row == 0, up[0:1], jnp.where(row == 1, up[1:2], pltpu.roll(u, 2, 0)))
    cw = cw_ref[...]
    conv = cw[0:1] * u2 + cw[1:2] * u1 + cw[2:3] * u
    o_ref[0] = _merge_project(bg_ref[...], conv, mg_ref[...], yb_ref[0], x_ref[0], gt_ref[0], wout_ref)
    tail_ref[0, 0] = u[tm - 2:tm]


def _mixer_long(z, yb, x, gt, halo_c, halo_x, conv_w, wout_bf, *, tm):
    bsz, seq, c = x.shape
    nt = seq // tm
    row = lambda b, i: b * nt + i
    zspec = lambda col: pl.BlockSpec((tm, c), lambda b, i, col=col: (row(b, i), col))
    return pl.pallas_call(
        _mixer_long_kernel,
        grid=(bsz, nt),
        in_specs=[zspec(COL_BG // c), zspec(COL_CG // c), zspec(COL_XIN // c),
                  pl.BlockSpec((tm, 2 * c), lambda b, i: (row(b, i), COL_MG // (2 * c))),
                  pl.BlockSpec((1, tm, c), lambda b, i: (b, i, 0)),
                  pl.BlockSpec((1, tm, c), lambda b, i: (b, i, 0)),
                  pl.BlockSpec((1, 1, c), lambda b, i: (b, 0, 0)),
                  pl.BlockSpec((1, 1, 2, c), lambda b, i: (b, i, 0, 0)),
                  pl.BlockSpec((1, 1, 2, c), lambda b, i: (b, i, 0, 0)),
                  pl.BlockSpec(conv_w.shape, lambda b, i: (0, 0)),
                  pl.BlockSpec(wout_bf.shape, lambda b, i: (0, 0))],
        out_specs=[pl.BlockSpec((1, tm, c), lambda b, i: (b, i, 0)),
                   pl.BlockSpec((1, 1, 2, c), lambda b, i: (b, i, 0, 0))],
        out_shape=[jax.ShapeDtypeStruct((bsz, seq, c), F32),
                   jax.ShapeDtypeStruct((bsz, nt, 2, c), F32)],
        compiler_params=_cparams(("parallel", "parallel")),
        name="mixer_long",
    )(z, z, z, z, yb, x, gt, halo_c, halo_x, conv_w, wout_bf)


def _mixer_short_kernel(bg_ref, cg_ref, xin_ref, mg_ref, yb_ref, x_ref, gt_ref, prev_ref, cw_ref, wout_ref,
                        o_ref, st_ref):
    u = cg_ref[...] * xin_ref[...]
    t_len = u.shape[0]
    ext = [prev_ref[0], prev_ref[1]] + [u[t] for t in range(t_len)]
    cw = cw_ref[...]
    conv = jnp.stack([cw[0:1] * ext[t] + cw[1:2] * ext[t + 1] + cw[2:3] * ext[t + 2] for t in range(t_len)])
    o_ref[...] = _merge_project(bg_ref[...], conv, mg_ref[...], yb_ref[...], x_ref[...], gt_ref[...], wout_ref)
    st_ref[0] = ext[t_len]
    st_ref[1] = ext[t_len + 1]


def _mixer_short(z3, yb3, x3, gt, prev, conv_w, wout_bf):
    t_len, ns, c = x3.shape
    zspec = lambda col: pl.BlockSpec((t_len, ns, c), lambda i, col=col: (0, 0, col))
    full3 = pl.BlockSpec((t_len, ns, c), lambda i: (0, 0, 0))
    return pl.pallas_call(
        _mixer_short_kernel,
        grid=(1,),
        in_specs=[zspec(COL_BG // c), zspec(COL_CG // c), zspec(COL_XIN // c),
                  pl.BlockSpec((t_len, ns, 2 * c), lambda i: (0, 0, COL_MG // (2 * c))),
                  full3, full3,
                  pl.BlockSpec((1, ns, c), lambda i: (0, 0, 0)),
                  pl.BlockSpec((2, ns, c), lambda i: (0, 0, 0)),
                  pl.BlockSpec(conv_w.shape, lambda i: (0, 0)),
                  pl.BlockSpec(wout_bf.shape, lambda i: (0, 0))],
        out_specs=[full3, pl.BlockSpec((2, ns, c), lambda i: (0, 0, 0))],
        out_shape=[jax.ShapeDtypeStruct((t_len, ns, c), F32), jax.ShapeDtypeStruct((2, ns, c), F32)],
        compiler_params=_cparams(("arbitrary",)),
        name="mixer_short",
    )(z3, z3, z3, z3, yb3, x3, gt, prev, conv_w, wout_bf)


def _ffn_finish(act, x, gt, wd_ref, gf_ref):
    shp = act.shape
    proj = _dot(act.reshape(-1, shp[-1]).astype(BF16), wd_ref[...]).reshape(shp[:-1] + (wd_ref.shape[1],))
    x2 = x + gt * proj
    ms = jnp.mean(x2 * x2, axis=-1, keepdims=True)
    return x2 * lax.rsqrt(ms + EPS) * gf_ref[...]


def _ffn_long_kernel(u_ref, v_ref, x_ref, gt_ref, hu_ref, cw_ref, wd_ref, gf_ref, o_ref):
    u = u_ref[...]
    up = hu_ref[0, 0]
    row = lax.broadcasted_iota(jnp.int32, u.shape, 0)
    u1 = jnp.where(row == 0, up[1:2], pltpu.roll(u, 1, 0))
    u2 = jnp.where(row == 0, up[0:1], jnp.where(row == 1, up[1:2], pltpu.roll(u, 2, 0)))
    cw = cw_ref[...]
    uc = cw[0:1] * u2 + cw[1:2] * u1 + cw[2:3] * u
    act = uc * _sigmoid(uc) * v_ref[...]
    o_ref[0] = _ffn_finish(act, x_ref[0], gt_ref[0], wd_ref, gf_ref)


def _ffn_long(uv, x1, gt, halo_u, conv_w, wd_bf, gf_row, *, tm):
    bsz, seq, d = x1.shape
    f = conv_w.shape[1]
    nt = seq // tm
    row = lambda b, i: b * nt + i
    return pl.pallas_call(
        _ffn_long_kernel,
        grid=(bsz, nt),
        in_specs=[pl.BlockSpec((tm, f), lambda b, i: (row(b, i), 0)),
                  pl.BlockSpec((tm, f), lambda b, i: (row(b, i), 1)),
                  pl.BlockSpec((1, tm, d), lambda b, i: (b, i, 0)),
                  pl.BlockSpec((1, 1, d), lambda b, i: (b, 0, 0)),
                  pl.BlockSpec((1, 1, 2, f), lambda b, i: (b, i, 0, 0)),
                  pl.BlockSpec(conv_w.shape, lambda b, i: (0, 0)),
                  pl.BlockSpec(wd_bf.shape, lambda b, i: (0, 0)),
                  pl.BlockSpec(gf_row.shape, lambda b, i: (0, 0))],
        out_specs=pl.BlockSpec((1, tm, d), lambda b, i: (b, i, 0)),
        out_shape=jax.ShapeDtypeStruct((bsz, seq, d), F32),
        compiler_params=_cparams(("parallel", "parallel")),
        name="ffn_long",
    )(uv, uv, x1, gt, halo_u, conv_w, wd_bf, gf_row)


def _ffn_short_kernel(u_ref, v_ref, x_ref, gt_ref, prev_ref, cw_ref, wd_ref, gf_ref, o_ref):
    u = u_ref[...]
    t_len = u.shape[0]
    ext = [prev_ref[0], prev_ref[1]] + [u[t] for t in range(t_len)]
    cw = cw_ref[...]
    uc = jnp.stack([cw[0:1] * ext[t] + cw[1:2] * ext[t + 1] + cw[2:3] * ext[t + 2] for t in range(t_len)])
    act = uc * _sigmoid(uc) * v_ref[...]
    o_ref[...] = _ffn_finish(act, x_ref[...], gt_ref[...], wd_ref, gf_ref)


def _ffn_short(uv3, x3, gt, prev, conv_w, wd_bf, gf_row):
    t_len, ns, d = x3.shape
    f = conv_w.shape[1]
    return pl.pallas_call(
        _ffn_short_kernel,
        grid=(1,),
        in_specs=[pl.BlockSpec((t_len, ns, f), lambda i: (0, 0, 0)),
                  pl.BlockSpec((t_len, ns, f), lambda i: (0, 0, 1)),
                  pl.BlockSpec((t_len, ns, d), lambda i: (0, 0, 0)),
                  pl.BlockSpec((1, ns, d), lambda i: (0, 0, 0)),
                  pl.BlockSpec((2, ns, f), lambda i: (0, 0, 0)),
                  pl.BlockSpec(conv_w.shape, lambda i: (0, 0)),
                  pl.BlockSpec(wd_bf.shape, lambda i: (0, 0)),
                  pl.BlockSpec(gf_row.shape, lambda i: (0, 0))],
        out_specs=pl.BlockSpec((t_len, ns, d), lambda i: (0, 0, 0)),
        out_shape=jax.ShapeDtypeStruct((t_len, ns, d), F32),
        compiler_params=_cparams(("arbitrary",)),
        name="ffn_short",
    )(uv3, uv3, x3, gt, prev, conv_w, wd_bf, gf_row)


def _pad_axis(a, axis, size):
    pad = [(0, 0)] * a.ndim
    pad[axis] = (0, size - a.shape[axis])
    return jnp.pad(a, pad)


def _split_weights(w_in):
    w_main = jnp.concatenate([w_in[:, 0:IN_OFF_KV],
                              w_in[:, IN_OFF_MG:IN_OFF_MG + 2 * D_MODEL],
                              w_in[:, IN_OFF_KV:IN_OFF_AG]], axis=1)
    pieces = []
    for g in range(N_KV):
        for c in range(3):
            off = IN_OFF_AG + c * N_KV * GROUP + g * GROUP
            pieces.append(w_in[:, off:off + GROUP])
        pieces.append(jnp.zeros((w_in.shape[0], LANES - 3 * GROUP), w_in.dtype))
    w_ag = jnp.concatenate(pieces, axis=1)
    return w_main.astype(BF16), w_ag.astype(BF16)


def kernel(x_prompt, x_sample, cache_cmp_k, cache_cmp_v, cache_slc_k, cache_slc_v, cache_win_k, cache_win_v,
           state_conv, state_ffn, page_table, c_prompt, c_sample, w_ada, b_ada, norm1_g, norm2_g, w_in, conv_w,
           cmp_pos_k, cmp_w1_k, cmp_b1_k, cmp_w2_k, cmp_pos_v, cmp_w1_v, cmp_b1_v, cmp_w2_v,
           w_out, w_up, ffn_conv_w, w_down, norm_f_g):
    depth = w_ada.shape[0]
    assert depth == 1, "single trunk layer"
    bsz, seq, d = x_prompt.shape
    ns, n_q, _ = x_sample.shape
    npages = page_table.shape[1]
    past = npages * PAGE
    wb = cache_win_k.shape[2]
    assert d == D_MODEL and seq % KEY_TILE == 0 and seq >= WIN_KEYS and n_q <= SUBLANES and past % SLC_BLK == 0

    w_main, w_ag = _split_weights(w_in[0])
    w_ada_bf = w_ada[0].astype(BF16)
    w_out_bf = w_out[0].astype(BF16)
    w_up_bf = w_up[0].astype(BF16)
    w_down_bf = w_down[0].astype(BF16)
    g1 = norm1_g[0].reshape(1, d)
    g2 = norm2_g[0].reshape(1, d)
    gf = norm_f_g.reshape(1, d)
    cw = conv_w[0]
    fcw = ffn_conv_w[0]
    wts_k = _compress_weights(cmp_pos_k[0], cmp_w1_k[0], cmp_b1_k[0], cmp_w2_k[0])
    wts_v = _compress_weights(cmp_pos_v[0], cmp_w1_v[0], cmp_b1_v[0], cmp_w2_v[0])

    n_c = bsz + ns
    n_c_pad = -(-n_c // SUBLANES) * SUBLANES
    c_all = _pad_axis(jnp.concatenate([c_prompt, c_sample], axis=0), 0, n_c_pad)
    mod = _adaln(c_all, w_ada_bf, b_ada[0].reshape(1, -1))
    mod_p = mod[:bsz].reshape(bsz, 6, 1, d)
    mod_s = mod[bsz:n_c].reshape(ns, 6, d)
    sh1_p, sc1_p, gt1_p, sh2_p, sc2_p, gt2_p = (mod_p[:, k] for k in range(6))
    sh1_s, sc1_s, gt1_s, sh2_s, sc2_s, gt2_s = (mod_s[:, k] for k in range(6))

    def kv_slices(z2d):
        return [z2d[:, COL_KV + k * KV_W:COL_KV + (k + 1) * KV_W] for k in range(6)]

    rows_p = bsz * seq
    tm_p = min(1024, seq)
    z_p, zag_p = _norm_mod_matmul(x_prompt.reshape(rows_p, d), g1, sc1_p, sh1_p, w_main, w_ag,
                                  tm=tm_p, tn=1280, rows_per_mod=seq, name="in_proj_prompt")
    kv_p = [a.reshape(bsz, seq, N_KV, HEAD_DIM) for a in kv_slices(z_p)]
    tab_p = jnp.arange(bsz * (seq // PAGE), dtype=jnp.int32)
    as_pages = lambda a: a.reshape(-1, PAGE, N_KV, HEAD_DIM).transpose(0, 2, 3, 1)[None]
    kc_p = _compress_paged(as_pages(kv_p[0]), tab_p, bsz, seq // PAGE, wts_k)
    vc_p = _compress_paged(as_pages(kv_p[1]), tab_p, bsz, seq // PAGE, wts_v)
    key_major = lambda a: a.transpose(0, 2, 1, 3).astype(BF16)
    feat_major = lambda a: a.transpose(0, 2, 3, 1).astype(BF16)
    qh = z_p[:, COL_Q:COL_Q + d].reshape(bsz, seq, N_KV, GROUP, HEAD_DIM).transpose(0, 2, 3, 1, 4).astype(BF16)
    n_slc_p = seq // SLC_BLK
    n_lanes_p = -(-n_slc_p // LANES) * LANES
    nch_p = seq // CMP_STRIDE
    ovl_p = _overlap(nch_p, nch_p - 1, n_slc_p, n_lanes_p).T
    yb_h = _prompt_attention(qh, kc_p, vc_p.transpose(0, 1, 3, 2), key_major(kv_p[2]), feat_major(kv_p[3]),
                             key_major(kv_p[4]), feat_major(kv_p[5]), zag_p.reshape(bsz, seq, AG_W), ovl_p, seq)
    yb_p = yb_h.transpose(0, 4, 1, 2, 3).reshape(bsz, seq, d)

    tm_m = min(512, seq)
    nt_m = seq // tm_m
    z4 = z_p.reshape(bsz, nt_m, tm_m, Z_W)
    conv0 = jnp.zeros((bsz, 1, 2, d), F32)
    halo_c = jnp.concatenate([conv0, z4[:, :-1, tm_m - 2:, COL_CG:COL_CG + d]], axis=1)
    halo_x = jnp.concatenate([jnp.ones((bsz, 1, 2, d), F32), z4[:, :-1, tm_m - 2:, COL_XIN:COL_XIN + d]], axis=1)
    x1_p, tail_p = _mixer_long(z_p, yb_p, x_prompt, gt1_p, halo_c, halo_x, cw, w_out_bf, tm=tm_m)
    conv_p = tail_p[:, -1]

    uv_p = _norm_mod_matmul(x1_p.reshape(rows_p, d), g2, sc2_p, sh2_p, w_up_bf, None,
                            tm=tm_p, tn=1408, rows_per_mod=seq, name="up_proj_prompt")
    tm_f = min(256, seq)
    nt_f = seq // tm_f
    u4 = uv_p.reshape(bsz, nt_f, tm_f, 2 * D_FF)
    halo_u = jnp.concatenate([jnp.zeros((bsz, 1, 2, D_FF), F32), u4[:, :-1, tm_f - 2:, :D_FF]], axis=1)
    y_prompt = _ffn_long(uv_p, x1_p, gt2_p, halo_u, fcw, w_down_bf, gf, tm=tm_f)
    ffn_p = u4[:, -1, tm_f - 2:, :D_FF]

    wbp = min(WINDOW, seq)
    outs_p = [a[None] for a in kv_p[:4]] + [kv_p[4][None, :, seq - wbp:], kv_p[5][None, :, seq - wbp:],
                                            conv_p[None], ffn_p[None]]

    rows_s = n_q * ns
    x_s3 = x_sample.transpose(1, 0, 2)
    tile_t = lambda a: jnp.tile(a, (n_q, 1))[None]
    z_s, zag_s = _norm_mod_matmul(x_s3.reshape(rows_s, d), g1, tile_t(sc1_s), tile_t(sh1_s), w_main, w_ag,
                                  tm=rows_s, tn=1280, rows_per_mod=rows_s, name="in_proj_sample")
    kv_s = [a.reshape(n_q, ns, N_KV, HEAD_DIM).transpose(1, 0, 2, 3) for a in kv_slices(z_s)]
    pool = lambda c: c.transpose(0, 1, 3, 4, 2)
    tab_s = page_table.reshape(-1).astype(jnp.int32)
    kc_s = _compress_paged(pool(cache_cmp_k), tab_s, ns, npages, wts_k)
    vc_s = _compress_paged(pool(cache_cmp_v), tab_s, ns, npages, wts_v)

    q_s = z_s[:, COL_Q:COL_Q + d].reshape(n_q, ns, N_KV, GROUP, HEAD_DIM)
    q1 = _pad_axis(q_s.transpose(1, 2, 3, 0, 4), 3, SUBLANES).reshape(ns, N_KV, GROUP * SUBLANES, HEAD_DIM)
    q2 = _pad_axis(q_s.transpose(1, 2, 0, 3, 4), 3, SUBLANES)
    n_slc_s = -(-(past + n_q) // SLC_BLK)
    n_lanes_s = -(-n_slc_s // LANES) * LANES
    nch_s = past // CMP_STRIDE
    ovl_s = _overlap(nch_s, nch_s - 1, n_slc_s, n_lanes_s)
    oc1, idx = _sample_cmp(q1.astype(BF16), kc_s, vc_s, ovl_s, past, n_slc_s)
    oc2 = _pad_axis(oc1.reshape(ns, N_KV, GROUP, SUBLANES, HEAD_DIM)[:, :, :, :n_q].transpose(0, 1, 3, 2, 4),
                    3, SUBLANES)

    pbk = past // SLC_BLK
    bpp = PAGE // SLC_BLK
    jp = jnp.minimum(idx[:, :, :n_q, :TOPK], pbk - 1)
    page = jnp.take_along_axis(page_table, (jp // bpp).reshape(ns, -1), axis=1).reshape(jp.shape)
    prow = (page * bpp + jp % bpp).reshape(-1).astype(jnp.int32)

    new_rows = [_pad_axis(a.transpose(0, 2, 1, 3), 2, SUBLANES) for a in kv_s[2:]]
    ag_s = zag_s.reshape(n_q, ns, N_KV, LANES)[..., :3 * GROUP].reshape(n_q, ns, N_KV, 3, GROUP)
    ag2 = _pad_axis(_pad_axis(ag_s.transpose(1, 2, 0, 4, 3), 3, SUBLANES), 4, LANES)
    ag2 = ag2.reshape(ns, N_KV, n_q * SUBLANES, LANES)
    y_h = _sample_selwin(prow, pool(cache_slc_k), pool(cache_slc_v), pool(cache_win_k), pool(cache_win_v),
                         idx, q2.astype(BF16), new_rows[0], new_rows[1], new_rows[2], new_rows[3], oc2, ag2, past)
    yb_s3 = y_h[:, :, :, :GROUP].transpose(2, 0, 1, 3, 4).reshape(n_q, ns, d)

    new_tail = lambda a: jnp.pad(a.transpose(0, 2, 3, 1), ((0, 0), (0, 0), (0, 0), (LANES - n_q, 0)))
    win_k_s = _win_shift(pool(cache_win_k), new_tail(kv_s[4]), n_q).transpose(0, 1, 4, 2, 3)
    win_v_s = _win_shift(pool(cache_win_v), new_tail(kv_s[5]), n_q).transpose(0, 1, 4, 2, 3)

    z_s3 = z_s.reshape(n_q, ns, Z_W)
    x1_s3, conv_s3 = _mixer_short(z_s3, yb_s3, x_s3, gt1_s[None], state_conv[0].transpose(1, 0, 2), cw, w_out_bf)
    uv_s = _norm_mod_matmul(x1_s3.reshape(rows_s, d), g2, tile_t(sc2_s), tile_t(sh2_s), w_up_bf, None,
                            tm=rows_s, tn=1408, rows_per_mod=rows_s, name="up_proj_sample")
    uv_s3 = uv_s.reshape(n_q, ns, 2 * D_FF)
    y_s3 = _ffn_short(uv_s3, x1_s3, gt2_s[None], state_ffn[0].transpose(1, 0, 2), fcw, w_down_bf, gf)
    y_sample = y_s3.transpose(1, 0, 2)
    ffn_ext = jnp.concatenate([state_ffn[0], uv_s3[:, :, :D_FF].transpose(1, 0, 2)], axis=1)
    ffn_s = ffn_ext[:, n_q:]
    conv_s = conv_s3.transpose(1, 0, 2)

    outs_s = [a[None] for a in kv_s[:4]] + [win_k_s, win_v_s, conv_s[None], ffn_s[None]]
    return (y_prompt, y_sample, *outs_p, *outs_s)
```

```python
import functools

import numpy as np
import jax
import jax.numpy as jnp
from jax import lax
from jax.experimental import pallas as pl
from jax.experimental.pallas import tpu as pltpu

F32 = jnp.float32
BF16 = jnp.bfloat16

D_MODEL = 1024
N_KV = 4
GROUP = 4
HEAD_DIM = 64
KV_W = N_KV * HEAD_DIM
GPAIRS = N_KV // 2
CMP_BLK = 32
CMP_STRIDE = 16
CMP_HID = 128
SLC_BLK = 64
TOPK = 16
WINDOW = 512
PAGE = 128
D_FF = 2816
EPS = 1e-6
NEG = -1e30
BIG = 1e30
SCALE = HEAD_DIM ** -0.5

LANES = 128
SUBLANES = 8
VMEM_LIMIT = 56 * 1024 * 1024

COL_BG, COL_CG, COL_XIN, COL_Q, COL_MG, COL_KV = 0, 1024, 2048, 3072, 4096, 6144
Z_W = 7680
AG_W = N_KV * LANES
IN_OFF_Q = 3072
IN_OFF_KV = 4096
IN_OFF_AG = 5632
IN_OFF_MG = 5680

KEY_TILE = 512
WIN_KEYS = WINDOW + 128
Q_BLK = 128
assert PAGE == 2 * SLC_BLK


def _cparams(sem, vmem=VMEM_LIMIT):
    return pltpu.CompilerParams(dimension_semantics=sem, vmem_limit_bytes=vmem)


def _dot(a, b):
    return jnp.dot(a, b, preferred_element_type=F32)


def _dot_nt(a, b):
    return lax.dot_general(a, b, (((1,), (1,)), ((), ())), preferred_element_type=F32)


def _sigmoid(x):
    return 1.0 / (1.0 + jnp.exp(-x))


def _softmax_parts(s, mask):
    s = jnp.where(mask, s, NEG)
    m = jnp.max(s, axis=-1, keepdims=True)
    return s, m


def _topk_select(score, pos, k, axis, want_idx=False):
    n = score.shape[axis]
    sel = jnp.zeros(score.shape, F32)
    idx_mat = jnp.zeros((score.shape[0], LANES), F32) if want_idx else None
    out_lane = lax.broadcasted_iota(jnp.int32, (score.shape[0], LANES), 1) if want_idx else None
    for i in range(k):
        m = jnp.max(score, axis=axis, keepdims=True)
        cand = jnp.where(score == m, pos, float(n))
        idx = jnp.min(cand, axis=axis, keepdims=True)
        hit = pos == idx
        sel = jnp.where(hit, 1.0, sel)
        score = jnp.where(hit, -jnp.inf, score)
        if want_idx:
            idx_mat = jnp.where(out_lane == i, idx, idx_mat)
    return sel, idx_mat


def _adaln_kernel(c_ref, w_ref, b_ref, o_ref):
    o_ref[...] = _dot(c_ref[...].astype(BF16), w_ref[...]) + b_ref[...]


def _adaln(c_all, w_bf, b_row):
    rows, d = c_all.shape
    n = w_bf.shape[1]
    tn = 1536
    return pl.pallas_call(
        _adaln_kernel,
        grid=(n // tn,),
        in_specs=[pl.BlockSpec((rows, d), lambda j: (0, 0)),
                  pl.BlockSpec((d, tn), lambda j: (0, j)),
                  pl.BlockSpec((1, tn), lambda j: (0, j))],
        out_specs=pl.BlockSpec((rows, tn), lambda j: (0, j)),
        out_shape=jax.ShapeDtypeStruct((rows, n), F32),
        compiler_params=_cparams(("parallel",)),
        name="adaln",
    )(c_all, w_bf, b_row)


def _norm_mod_mm_kernel(*refs, has_gate):
    if has_gate:
        x_ref, g_ref, sc_ref, sh_ref, w_ref, wg_ref, o_ref, og_ref, h_ref = refs
    else:
        x_ref, g_ref, sc_ref, sh_ref, w_ref, o_ref, h_ref = refs

    @pl.when(pl.program_id(1) == 0)
    def _():
        x = x_ref[...]
        ms = jnp.mean(x * x, axis=-1, keepdims=True)
        y = x * lax.rsqrt(ms + EPS) * g_ref[...]
        h = (y * (1.0 + sc_ref[0]) + sh_ref[0]).astype(BF16)
        h_ref[...] = h
        if has_gate:
            og_ref[...] = _dot(h, wg_ref[...])

    o_ref[...] = _dot(h_ref[...], w_ref[...])


def _norm_mod_matmul(x2d, g_row, sc, sh, w_bf, wg_bf, *, tm, tn, rows_per_mod, name):
    rows, d = x2d.shape
    n = w_bf.shape[1]
    per_row = sc.shape[1] != 1
    if per_row:
        mod_spec = pl.BlockSpec((1, tm, d), lambda i, j: (0, i, 0))
    else:
        mod_spec = pl.BlockSpec((1, 1, d), lambda i, j: ((i * tm) // rows_per_mod, 0, 0))
    has_gate = wg_bf is not None
    in_specs = [pl.BlockSpec((tm, d), lambda i, j: (i, 0)),
                pl.BlockSpec((1, d), lambda i, j: (0, 0)),
                mod_spec, mod_spec,
                pl.BlockSpec((d, tn), lambda i, j: (0, j))]
    out_specs = [pl.BlockSpec((tm, tn), lambda i, j: (i, j))]
    out_shape = [jax.ShapeDtypeStruct((rows, n), F32)]
    args = [x2d, g_row, sc, sh, w_bf]
    if has_gate:
        ng = wg_bf.shape[1]
        in_specs.append(pl.BlockSpec((d, ng), lambda i, j: (0, 0)))
        out_specs.append(pl.BlockSpec((tm, ng), lambda i, j: (i, 0)))
        out_shape.append(jax.ShapeDtypeStruct((rows, ng), F32))
        args.append(wg_bf)
    res = pl.pallas_call(
        functools.partial(_norm_mod_mm_kernel, has_gate=has_gate),
        grid=(rows // tm, n // tn),
        in_specs=in_specs,
        out_specs=out_specs,
        out_shape=out_shape,
        scratch_shapes=[pltpu.VMEM((tm, d), BF16)],
        compiler_params=_cparams(("parallel", "arbitrary")),
        name=name,
    )(*args)
    return res if has_gate else res[0]


def _compress_kernel(tab_ref, pool_ref, wp_ref, pos_ref, w1_ref, b1_ref, w2_ref, o_ref, xt_ref, x_ref, sem,
                     *, npages):
    step = pl.program_id(0) * GPAIRS + pl.program_id(1)
    nsteps = pl.num_programs(0) * GPAIRS
    slot = lax.rem(step, 2)
    nch = npages * (PAGE // CMP_STRIDE)
    ntok = npages * PAGE

    def page_copy(st, sl, p):
        page = tab_ref[lax.div(st, GPAIRS) * npages + p]
        return pltpu.make_async_copy(pool_ref.at[0, page, lax.rem(st, GPAIRS)],
                                     xt_ref.at[sl, :, pl.ds(pl.multiple_of(p * PAGE, PAGE), PAGE)],
                                     sem.at[sl])

    def start_all(st, sl):
        def body(p, carry):
            page_copy(st, sl, p).start()
            return carry
        lax.fori_loop(0, npages, body, 0)

    def wait_all(st, sl):
        def body(p, carry):
            page_copy(st, sl, p).wait()
            return carry
        lax.fori_loop(0, npages, body, 0)

    @pl.when(step == 0)
    def _():
        start_all(0, 0)

    @pl.when(step + 1 < nsteps)
    def _():
        start_all(step + 1, 1 - slot)

    wait_all(step, slot)

    slab = min(ntok, 512)
    for j in range(ntok // slab):
        x_ref[pl.ds(j * slab, slab), :] = xt_ref[slot, :, pl.ds(j * slab, slab)].T

    const = _dot(pos_ref[...], w1_ref[...])[0:1, :] + b1_ref[...]
    low = lax.broadcasted_iota(jnp.int32, (nch, 2 * HEAD_DIM), 1) < HEAD_DIM
    accs = [None, None]
    for s2 in range(CMP_STRIDE // 2):
        a0 = x_ref[pl.ds(2 * s2, nch, stride=CMP_STRIDE), :]
        a1 = x_ref[pl.ds(2 * s2 + 1, nch, stride=CMP_STRIDE), :]
        pairs = (jnp.where(low, a0, pltpu.roll(a1, HEAD_DIM, 1)),
                 jnp.where(low, pltpu.roll(a0, HEAD_DIM, 1), a1))
        for gi in range(2):
            part = _dot(pairs[gi].astype(BF16), wp_ref[s2])
            accs[gi] = part if accs[gi] is None else accs[gi] + part
    rows = lax.broadcasted_iota(jnp.int32, (nch, HEAD_DIM), 0)
    for gi in range(2):
        first = accs[gi][:, :CMP_HID]
        second = accs[gi][:, CMP_HID:]
        hid = first + pltpu.roll(second, nch - 1, 0) + const
        act = hid * _sigmoid(hid)
        out = _dot(act.astype(BF16), w2_ref[...])
        o_ref[0, gi] = jnp.where(rows < nch - 1, out, 0.0)


def _compress_paged(pool_t, table_flat, nseq, npages, wts):
    wp, posb, w1b, b1r, w2b = wts
    nch = npages * (PAGE // CMP_STRIDE)
    ntok = npages * PAGE
    pool_pairs = pool_t.reshape(1, -1, GPAIRS, 2 * HEAD_DIM, PAGE)
    grid_spec = pltpu.PrefetchScalarGridSpec(
        num_scalar_prefetch=1,
        grid=(nseq, GPAIRS),
        in_specs=[pl.BlockSpec(memory_space=pl.ANY),
                  pl.BlockSpec(wp.shape, lambda b, g, t: (0, 0, 0)),
                  pl.BlockSpec(posb.shape, lambda b, g, t: (0, 0)),
                  pl.BlockSpec(w1b.shape, lambda b, g, t: (0, 0)),
                  pl.BlockSpec(b1r.shape, lambda b, g, t: (0, 0)),
                  pl.BlockSpec(w2b.shape, lambda b, g, t: (0, 0))],
        out_specs=pl.BlockSpec((1, 2, nch, HEAD_DIM), lambda b, g, t: (b, g, 0, 0)),
        scratch_shapes=[pltpu.VMEM((2, 2 * HEAD_DIM, ntok), F32),
                        pltpu.VMEM((ntok, 2 * HEAD_DIM), F32),
                        pltpu.SemaphoreType.DMA((2,))],
    )
    return pl.pallas_call(
        functools.partial(_compress_kernel, npages=npages),
        grid_spec=grid_spec,
        out_shape=jax.ShapeDtypeStruct((nseq, N_KV, nch, HEAD_DIM), F32),
        compiler_params=_cparams(("arbitrary", "arbitrary")),
        name="compress",
    )(table_flat, pool_pairs, wp, posb, w1b, b1r, w2b)


def _compress_weights(pos, w1, b1, w2):
    w1r = w1.reshape(2, CMP_STRIDE, HEAD_DIM, CMP_HID)
    cat = jnp.concatenate([w1r[0], w1r[1]], axis=-1)
    wp = cat.reshape(CMP_STRIDE // 2, 2 * HEAD_DIM, 2 * CMP_HID)
    posb = jnp.broadcast_to(pos.reshape(1, -1), (SUBLANES, CMP_BLK * HEAD_DIM))
    return (wp.astype(BF16), posb.astype(BF16), w1.astype(BF16), b1.reshape(1, -1), w2.astype(BF16))


def _col_softmax_parts(s, mask, col_live):
    s = jnp.where(mask, s, NEG)
    e = jnp.exp(s - jnp.max(s, axis=0, keepdims=True))
    scale = jnp.where(col_live, 1.0 / jnp.maximum(jnp.sum(e, axis=0, keepdims=True), 1e-30), 0.0)
    return e, scale


def _prompt_attn_kernel(q_ref, kc_ref, vct_ref, ks_ref, vst_ref, kw_ref, vwt_ref, ag_ref, ovl_ref, o_ref,
                        bias_ref, *, seq_len, n_slc, n_lanes):
    i = pl.program_id(2)
    s0 = i * Q_BLK
    cols = GROUP * Q_BLK
    q = (q_ref[0, 0].reshape(cols, HEAD_DIM).astype(F32) * SCALE).astype(BF16)

    def qpos_of(shape):
        return s0 + (lax.broadcasted_iota(jnp.int32, shape, 1) & (Q_BLK - 1))

    nch = kc_ref.shape[2]
    s = _dot_nt(kc_ref[0, 0].astype(BF16), q)
    blk_end = lax.broadcasted_iota(jnp.int32, (nch, cols), 0) * CMP_STRIDE + (CMP_BLK - 1)
    e, scale = _col_softmax_parts(s, blk_end <= qpos_of((nch, cols)), qpos_of((1, cols)) >= CMP_BLK - 1)
    o_cmp = _dot(vct_ref[0, 0].astype(BF16), e.astype(BF16)) * scale

    psum = e[:, 0:Q_BLK] * scale[:, 0:Q_BLK]
    for r in range(1, GROUP):
        psum = psum + e[:, r * Q_BLK:(r + 1) * Q_BLK] * scale[:, r * Q_BLK:(r + 1) * Q_BLK]
    p_hi = psum.astype(BF16)
    p_lo = (psum - p_hi.astype(F32)).astype(BF16)
    ovl_t = ovl_ref[...]
    imp_t = _dot(ovl_t, p_hi) + _dot(ovl_t, p_lo)
    j = lax.broadcasted_iota(jnp.int32, (n_lanes, Q_BLK), 0)
    qp = s0 + lax.broadcasted_iota(jnp.int32, (n_lanes, Q_BLK), 1)
    tb = lax.shift_right_logical(qp, 6)
    forced = (j == 0) | (j == tb) | (j == tb - 1)
    score = jnp.where(forced, BIG, jnp.where(j * SLC_BLK <= qp, imp_t, NEG))
    score = jnp.where(j < n_slc, score, -jnp.inf)
    sel_t, _ = _topk_select(score, j.astype(F32), min(TOPK, n_slc), 0)
    bias = jnp.where(sel_t > 0.5, 0.0, NEG)
    bias_ref[...] = jnp.concatenate([bias] * GROUP, axis=1)

    blocks_per_tile = KEY_TILE // SLC_BLK

    def scores(t):
        k0 = pl.multiple_of(t * KEY_TILE, KEY_TILE)
        return _dot_nt(ks_ref[0, 0, pl.ds(k0, KEY_TILE), :], q)

    def update(t, sc, state, causal):
        m_old, l_old, acc = state
        k0 = pl.multiple_of(t * KEY_TILE, KEY_TILE)
        bt = bias_ref[pl.ds(pl.multiple_of(t * blocks_per_tile, blocks_per_tile), blocks_per_tile), :]
        sc = (sc.reshape(blocks_per_tile, SLC_BLK, cols) + bt[:, None, :]).reshape(KEY_TILE, cols)
        if causal:
            kpos = k0 + lax.broadcasted_iota(jnp.int32, (KEY_TILE, cols), 0)
            sc = jnp.where(kpos <= qpos_of((KEY_TILE, cols)), sc, NEG)
        m_new = jnp.maximum(m_old, jnp.max(sc, axis=0, keepdims=True))
        pt = jnp.exp(sc - m_new)
        alpha = jnp.exp(m_old - m_new)
        l_new = alpha * l_old + jnp.sum(pt, axis=0, keepdims=True)
        acc = alpha * acc + _dot(vst_ref[0, 0, :, pl.ds(k0, KEY_TILE)], pt.astype(BF16))
        return m_new, l_new, acc

    def full_tile(t, carry):
        nxt = scores(t + 1)
        state = update(t, carry[0], carry[1:], False)
        return (nxt,) + state

    n_full = s0 // KEY_TILE
    init = (scores(0), jnp.full((1, cols), NEG, F32), jnp.zeros((1, cols), F32),
            jnp.zeros((HEAD_DIM, cols), F32))
    carry = lax.fori_loop(0, n_full, full_tile, init)
    _, l_sel, acc_sel = update(n_full, carry[0], carry[1:], True)
    o_sel = acc_sel / jnp.maximum(l_sel, 1e-30)

    w0 = pl.multiple_of(jnp.clip(s0 - WINDOW, 0, seq_len - WIN_KEYS), Q_BLK)
    sw = _dot_nt(kw_ref[0, 0, pl.ds(w0, WIN_KEYS), :], q)
    kpos = w0 + lax.broadcasted_iota(jnp.int32, (WIN_KEYS, cols), 0)
    qpw = qpos_of((WIN_KEYS, cols))
    wmask = jnp.where(kpos <= qpw, jnp.where(kpos > qpw - WINDOW, 1.0, 0.0), 0.0) > 0.5
    ew, wscale = _col_softmax_parts(sw, wmask, True)
    o_win = _dot(vwt_ref[0, 0, :, pl.ds(w0, WIN_KEYS)], ew.astype(BF16)) * wscale

    gate = _sigmoid(ag_ref[0]).T
    for r in range(GROUP):
        cs = slice(r * Q_BLK, (r + 1) * Q_BLK)
        y = (gate[r:r + 1] * o_cmp[:, cs]
             + gate[GROUP + r:GROUP + r + 1] * o_sel[:, cs]
             + gate[2 * GROUP + r:2 * GROUP + r + 1] * o_win[:, cs])
        o_ref[0, 0, r] = y


def _prompt_attention(qh, kc, vct, ks, vst, kw, vwt, zag, ovl, seq_len):
    bsz = qh.shape[0]
    nch = kc.shape[2]
    n_slc = seq_len // SLC_BLK
    n_lanes = ovl.shape[0]
    cols = GROUP * Q_BLK
    k_spec = pl.BlockSpec((1, 1, seq_len, HEAD_DIM), lambda b, g, i: (b, g, 0, 0))
    vt_spec = pl.BlockSpec((1, 1, HEAD_DIM, seq_len), lambda b, g, i: (b, g, 0, 0))
    return pl.pallas_call(
        functools.partial(_prompt_attn_kernel, seq_len=seq_len, n_slc=n_slc, n_lanes=n_lanes),
        grid=(bsz, N_KV, seq_len // Q_BLK),
        in_specs=[pl.BlockSpec((1, 1, GROUP, Q_BLK, HEAD_DIM), lambda b, g, i: (b, g, 0, i, 0)),
                  pl.BlockSpec((1, 1, nch, HEAD_DIM), lambda b, g, i: (b, g, 0, 0)),
                  pl.BlockSpec((1, 1, HEAD_DIM, nch), lambda b, g, i: (b, g, 0, 0)),
                  k_spec, vt_spec, k_spec, vt_spec,
                  pl.BlockSpec((1, Q_BLK, LANES), lambda b, g, i: (b, i, g)),
                  pl.BlockSpec(ovl.shape, lambda b, g, i: (0, 0))],
        out_specs=pl.BlockSpec((1, 1, GROUP, HEAD_DIM, Q_BLK), lambda b, g, i: (b, g, 0, 0, i)),
        out_shape=jax.ShapeDtypeStruct((bsz, N_KV, GROUP, HEAD_DIM, seq_len), F32),
        scratch_shapes=[pltpu.VMEM((n_lanes, cols), F32)],
        compiler_params=_cparams(("parallel", "parallel", "arbitrary")),
        name="prompt_attn",
    )(qh, kc, vct, ks, vst, kw, vwt, zag, ovl)


def _overlap(n_rows, n_cmp, n_slc, n_lanes):
    i = np.arange(n_rows)[:, None] * CMP_STRIDE
    jj = np.arange(n_lanes)[None, :] * SLC_BLK
    ovl = (i <= jj + SLC_BLK - 1) & (i + CMP_BLK - 1 >= jj)
    ovl &= (np.arange(n_rows)[:, None] < n_cmp) & (np.arange(n_lanes)[None, :] < n_slc)
    return jnp.asarray(ovl.astype(np.float32), dtype=BF16)


def _sample_cmp_kernel(q_ref, kc_ref, vc_ref, ovl_ref, oc_ref, idx_ref, *, past, n_slc, n_lanes):
    nch = kc_ref.shape[2]
    rows = GROUP * SUBLANES
    ovl = ovl_ref[...]
    imps = []
    for g in range(N_KV):
        q = (q_ref[0, g].astype(F32) * SCALE).astype(BF16)
        kc = kc_ref[0, g].astype(BF16)
        vc = vc_ref[0, g].astype(BF16)
        s = _dot_nt(q, kc)
        qpos = past + (lax.broadcasted_iota(jnp.int32, (rows, nch), 0) & (SUBLANES - 1))
        blk_end = lax.broadcasted_iota(jnp.int32, (rows, nch), 1) * CMP_STRIDE + (CMP_BLK - 1)
        mask = blk_end <= qpos
        s, m = _softmax_parts(s, mask)
        e = jnp.where(mask, jnp.exp(s - m), 0.0)
        p = e / jnp.maximum(jnp.sum(e, axis=-1, keepdims=True), 1e-30)
        oc_ref[0, g] = _dot(p.astype(BF16), vc)
        psum = p[0:SUBLANES]
        for r in range(1, GROUP):
            psum = psum + p[r * SUBLANES:(r + 1) * SUBLANES]
        p_hi = psum.astype(BF16)
        p_lo = (psum - p_hi.astype(F32)).astype(BF16)
        imps.append(_dot(p_hi, ovl) + _dot(p_lo, ovl))
    imp = jnp.concatenate(imps, axis=0)
    sel_rows = N_KV * SUBLANES
    j = lax.broadcasted_iota(jnp.int32, (sel_rows, n_lanes), 1)
    qp = past + (lax.broadcasted_iota(jnp.int32, (sel_rows, n_lanes), 0) & (SUBLANES - 1))
    tb = lax.shift_right_logical(qp, 6)
    forced = (j == 0) | (j == tb) | (j == tb - 1)
    score = jnp.where(forced, BIG, jnp.where(j * SLC_BLK <= qp, imp, NEG))
    score = jnp.where(j < n_slc, score, -jnp.inf)
    _, idx = _topk_select(score, j.astype(F32), TOPK, 1, True)
    idx_ref[0] = idx.astype(jnp.int32).reshape(N_KV, SUBLANES, LANES)


def _sample_cmp(q1, kc, vc, ovl, past, n_slc):
    ns = q1.shape[0]
    nch = kc.shape[2]
    rows = GROUP * SUBLANES
    return pl.pallas_call(
        functools.partial(_sample_cmp_kernel, past=past, n_slc=n_slc, n_lanes=ovl.shape[1]),
        grid=(ns,),
        in_specs=[pl.BlockSpec((1, N_KV, rows, HEAD_DIM), lambda b: (b, 0, 0, 0)),
                  pl.BlockSpec((1, N_KV, nch, HEAD_DIM), lambda b: (b, 0, 0, 0)),
                  pl.BlockSpec((1, N_KV, nch, HEAD_DIM), lambda b: (b, 0, 0, 0)),
                  pl.BlockSpec(ovl.shape, lambda b: (0, 0))],
        out_specs=[pl.BlockSpec((1, N_KV, rows, HEAD_DIM), lambda b: (b, 0, 0, 0)),
                   pl.BlockSpec((1, N_KV, SUBLANES, LANES), lambda b: (b, 0, 0, 0))],
        out_shape=[jax.ShapeDtypeStruct((ns, N_KV, rows, HEAD_DIM), F32),
                   jax.ShapeDtypeStruct((ns, N_KV, SUBLANES, LANES), jnp.int32)],
        compiler_params=_cparams(("parallel",)),
        name="sample_cmp",
    )(q1, kc, vc, ovl)


def _sample_selwin_kernel(page_ref, sk_ref, sv_ref, wk_ref, wv_ref, idx_ref, q_ref, ksn_ref, vsn_ref,
                          kwn_ref, vwn_ref, oc_ref, ag_ref, o_ref, kbuf, vbuf, sem,
                          *, n_q, past, wb):
    b = pl.program_id(0)
    g = pl.program_id(1)
    step = b * N_KV + g
    nsteps = pl.num_programs(0) * N_KV
    slot = lax.rem(step, 2)
    pb = past // SLC_BLK
    nkeys = TOPK * PAGE

    def copies(st, sl):
        sg = lax.rem(st, N_KV)
        res = []
        for qi in range(n_q):
            for k in range(TOPK):
                page = page_ref[(st * n_q + qi) * TOPK + k]
                dst = pl.ds(k * PAGE, PAGE)
                res.append(pltpu.make_async_copy(sk_ref.at[0, page, sg], kbuf.at[sl, qi, :, dst], sem.at[sl]))
                res.append(pltpu.make_async_copy(sv_ref.at[0, page, sg], vbuf.at[sl, qi, :, dst], sem.at[sl]))
        return res

    @pl.when(step == 0)
    def _():
        for cp in copies(0, 0):
            cp.start()

    @pl.when(step + 1 < nsteps)
    def _():
        for cp in copies(step + 1, 1 - slot):
            cp.start()

    for cp in copies(step, slot):
        cp.wait()

    rows = n_q * SUBLANES
    gate = _sigmoid(ag_ref[0, 0])
    ksn = ksn_ref[0, 0].astype(BF16)
    vsn = vsn_ref[0, 0].astype(BF16)
    new_t = lax.broadcasted_iota(jnp.int32, (SUBLANES, SUBLANES), 1)
    lane_k = lax.broadcasted_iota(jnp.int32, (SUBLANES, LANES), 1)
    slot_row = lax.broadcasted_iota(jnp.int32, (LANES, nkeys), 0)
    key_col = lax.broadcasted_iota(jnp.int32, (LANES, nkeys), 1)
    same_slot = slot_row == lax.shift_right_logical(key_col, 7)
    upper = (lax.shift_right_logical(key_col, 6) & 1) == 1
    expand_lo = jnp.where(same_slot, jnp.where(upper, 0.0, 1.0), 0.0).astype(BF16)
    expand_hi = jnp.where(same_slot, jnp.where(upper, 1.0, 0.0), 0.0).astype(BF16)
    idx_all = idx_ref[0, 0]

    o_sel_rows = []
    for qi in range(n_q):
        qq = (q_ref[0, 0, qi].astype(F32) * SCALE).astype(BF16)
        kg = kbuf[slot, qi].astype(BF16)
        vg = vbuf[slot, qi].astype(BF16)
        idx_row = jnp.broadcast_to(idx_all[qi:qi + 1, :], (SUBLANES, LANES))
        valid_k = jnp.where(lane_k < TOPK, 1.0, 0.0)
        in_past = jnp.where(idx_row < pb, valid_k, 0.0)
        odd = (jnp.minimum(idx_row, pb - 1) & 1) == 1
        is_new = jnp.where(idx_row == pb, valid_k, 0.0)
        has_new = jnp.max(is_new, axis=-1, keepdims=True)
        sel_lo = jnp.where(odd, 0.0, in_past).astype(BF16)
        sel_hi = jnp.where(odd, in_past, 0.0).astype(BF16)
        m1 = (_dot(sel_lo, expand_lo) + _dot(sel_hi, expand_hi)) > 0.5
        s1 = jnp.where(m1, _dot(qq, kg), NEG)
        m2 = jnp.where(new_t <= qi, jnp.where(new_t < n_q, has_new, 0.0), 0.0) > 0.5
        s2 = jnp.where(m2, _dot_nt(qq, ksn), NEG)
        mx = jnp.maximum(jnp.max(s1, axis=-1, keepdims=True), jnp.max(s2, axis=-1, keepdims=True))
        e1 = jnp.where(m1, jnp.exp(s1 - mx), 0.0)
        e2 = jnp.where(m2, jnp.exp(s2 - mx), 0.0)
        den = jnp.maximum(jnp.sum(e1, axis=-1, keepdims=True) + jnp.sum(e2, axis=-1, keepdims=True), 1e-30)
        o = _dot_nt((e1 / den).astype(BF16), vg) + _dot((e2 / den).astype(BF16), vsn)
        o_sel_rows.append(o)
    o_sel = jnp.concatenate(o_sel_rows, axis=0)

    qa = (q_ref[0, 0].reshape(rows, HEAD_DIM).astype(F32) * SCALE).astype(BF16)
    kwc = wk_ref[0, 0, 0].astype(BF16)
    vwc = wv_ref[0, 0, 0].astype(BF16)
    kwn = kwn_ref[0, 0].astype(BF16)
    vwn = vwn_ref[0, 0].astype(BF16)
    qpos = past + lax.shift_right_logical(lax.broadcasted_iota(jnp.int32, (rows, wb), 0), 3)
    kpos = (past - wb) + lax.broadcasted_iota(jnp.int32, (rows, wb), 1)
    mw1 = jnp.where(kpos <= qpos, jnp.where(kpos > qpos - WINDOW, jnp.where(kpos >= 0, 1.0, 0.0), 0.0), 0.0) > 0.5
    sw1 = jnp.where(mw1, _dot(qa, kwc), NEG)
    qi_n = lax.shift_right_logical(lax.broadcasted_iota(jnp.int32, (rows, SUBLANES), 0), 3)
    t_n = lax.broadcasted_iota(jnp.int32, (rows, SUBLANES), 1)
    mw2 = jnp.where(t_n <= qi_n, jnp.where(t_n < n_q, jnp.where(t_n > qi_n - WINDOW, 1.0, 0.0), 0.0), 0.0) > 0.5
    sw2 = jnp.where(mw2, _dot_nt(qa, kwn), NEG)
    mx = jnp.maximum(jnp.max(sw1, axis=-1, keepdims=True), jnp.max(sw2, axis=-1, keepdims=True))
    e1 = jnp.where(mw1, jnp.exp(sw1 - mx), 0.0)
    e2 = jnp.where(mw2, jnp.exp(sw2 - mx), 0.0)
    den = jnp.maximum(jnp.sum(e1, axis=-1, keepdims=True) + jnp.sum(e2, axis=-1, keepdims=True), 1e-30)
    o_win = _dot_nt((e1 / den).astype(BF16), vwc) + _dot((e2 / den).astype(BF16), vwn)

    o_cmp = oc_ref[0, 0].reshape(rows, HEAD_DIM)
    y = gate[:, 0:1] * o_cmp + gate[:, 1:2] * o_sel + gate[:, 2:3] * o_win
    o_ref[0, 0] = y.reshape(n_q, SUBLANES, HEAD_DIM)


def _sample_selwin(pages, slc_k, slc_v, win_k, win_v, idx, q2, ksn, vsn, kwn, vwn, oc2, ag2, past):
    ns, _, n_q = q2.shape[:3]
    wb = win_k.shape[-1]
    rows = n_q * SUBLANES
    nkeys = TOPK * PAGE
    any_spec = pl.BlockSpec(memory_space=pl.ANY)
    win_spec = pl.BlockSpec((1, 1, 1, HEAD_DIM, wb), lambda b, g, *_: (0, b, g, 0, 0))
    new_spec = pl.BlockSpec((1, 1, SUBLANES, HEAD_DIM), lambda b, g, *_: (b, g, 0, 0))
    qspec = pl.BlockSpec((1, 1, n_q, SUBLANES, HEAD_DIM), lambda b, g, *_: (b, g, 0, 0, 0))
    grid_spec = pltpu.PrefetchScalarGridSpec(
        num_scalar_prefetch=1,
        grid=(ns, N_KV),
        in_specs=[any_spec, any_spec, win_spec, win_spec,
                  pl.BlockSpec((1, 1, SUBLANES, LANES), lambda b, g, *_: (b, g, 0, 0)),
                  qspec, new_spec, new_spec, new_spec, new_spec, qspec,
                  pl.BlockSpec((1, 1, rows, LANES), lambda b, g, *_: (b, g, 0, 0))],
        out_specs=qspec,
        scratch_shapes=[pltpu.VMEM((2, n_q, HEAD_DIM, nkeys), F32),
                        pltpu.VMEM((2, n_q, HEAD_DIM, nkeys), F32),
                        pltpu.SemaphoreType.DMA((2,))],
    )
    return pl.pallas_call(
        functools.partial(_sample_selwin_kernel, n_q=n_q, past=past, wb=wb),
        grid_spec=grid_spec,
        out_shape=jax.ShapeDtypeStruct((ns, N_KV, n_q, SUBLANES, HEAD_DIM), F32),
        compiler_params=_cparams(("arbitrary", "arbitrary")),
        name="sample_selwin",
    )(pages, slc_k, slc_v, win_k, win_v, idx, q2, ksn, vsn, kwn, vwn, oc2, ag2)


def _win_shift_kernel(w_ref, n_ref, o_ref, *, n_new):
    wb = w_ref.shape[-1]
    lane = lax.broadcasted_iota(jnp.int32, (HEAD_DIM, LANES), 1)
    for g in range(N_KV):
        shifted = pltpu.roll(w_ref[0, 0, g], wb - n_new, 1)
        o_ref[0, 0, g] = shifted
        o_ref[0, 0, g, :, wb - LANES:wb] = jnp.where(lane >= LANES - n_new, n_ref[0, g], shifted[:, wb - LANES:wb])


def _win_shift(win_t, new_t, n_new):
    ns, wb = win_t.shape[1], win_t.shape[-1]
    return pl.pallas_call(
        functools.partial(_win_shift_kernel, n_new=n_new),
        grid=(ns,),
        in_specs=[pl.BlockSpec((1, 1, N_KV, HEAD_DIM, wb), lambda b: (0, b, 0, 0, 0)),
                  pl.BlockSpec((1, N_KV, HEAD_DIM, LANES), lambda b: (b, 0, 0, 0))],
        out_specs=pl.BlockSpec((1, 1, N_KV, HEAD_DIM, wb), lambda b: (0, b, 0, 0, 0)),
        out_shape=jax.ShapeDtypeStruct(win_t.shape, F32),
        compiler_params=_cparams(("parallel",)),
        name="win_shift",
    )(win_t, new_t)


def _merge_project(bg, conv, mg, yb, x, gt, wout_ref):
    ya = bg * conv
    c = ya.shape[-1]
    mix = _sigmoid(mg[..., :c]) * ya + _sigmoid(mg[..., c:]) * yb
    shp = mix.shape
    proj = _dot(mix.reshape(-1, c).astype(BF16), wout_ref[...]).reshape(shp[:-1] + (wout_ref.shape[1],))
    return x + gt * proj


def _mixer_long_kernel(bg_ref, cg_ref, xin_ref, mg_ref, yb_ref, x_ref, gt_ref, hc_ref, hx_ref, cw_ref,
                       wout_ref, o_ref, tail_ref):
    u = cg_ref[...] * xin_ref[...]
    tm = u.shape[0]
    up = hc_ref[0, 0] * hx_ref[0, 0]
    row = lax.broadcasted_iota(jnp.int32, u.shape, 0)
    u1 = jnp.where(row == 0, up[1:2], pltpu.roll(u, 1, 0))
    u2 = jnp.where(row == 0, up[0:1], jnp.where(row == 1, up[1:2], pltpu.roll(u, 2, 0)))
    cw = cw_ref[...]
    conv = cw[0:1] * u2 + cw[1:2] * u1 + cw[2:3] * u
    o_ref[0] = _merge_project(bg_ref[...], conv, mg_ref[...], yb_ref[0], x_ref[0], gt_ref[0], wout_ref)
    tail_ref[0, 0] = u[tm - 2:tm]


def _mixer_long(z, yb, x, gt, halo_c, halo_x, conv_w, wout_bf, *, tm):
    bsz, seq, c = x.shape
    nt = seq // tm
    row = lambda b, i: b * nt + i
    zspec = lambda col: pl.BlockSpec((tm, c), lambda b, i, col=col: (row(b, i), col))
    return pl.pallas_call(
        _mixer_long_kernel,
        grid=(bsz, nt),
        in_specs=[zspec(COL_BG // c), zspec(COL_CG // c), zspec(COL_XIN // c),
                  pl.BlockSpec((tm, 2 * c), lambda b, i: (row(b, i), COL_MG // (2 * c))),
                  pl.BlockSpec((1, tm, c), lambda b, i: (b, i, 0)),
                  pl.BlockSpec((1, tm, c), lambda b, i: (b, i, 0)),
                  pl.BlockSpec((1, 1, c), lambda b, i: (b, 0, 0)),
                  pl.BlockSpec((1, 1, 2, c), lambda b, i: (b, i, 0, 0)),
                  pl.BlockSpec((1, 1, 2, c), lambda b, i: (b, i, 0, 0)),
                  pl.BlockSpec(conv_w.shape, lambda b, i: (0, 0)),
                  pl.BlockSpec(wout_bf.shape, lambda b, i: (0, 0))],
        out_specs=[pl.BlockSpec((1, tm, c), lambda b, i: (b, i, 0)),
                   pl.BlockSpec((1, 1, 2, c), lambda b, i: (b, i, 0, 0))],
        out_shape=[jax.ShapeDtypeStruct((bsz, seq, c), F32),
                   jax.ShapeDtypeStruct((bsz, nt, 2, c), F32)],
        compiler_params=_cparams(("parallel", "parallel")),
        name="mixer_long",
    )(z, z, z, z, yb, x, gt, halo_c, halo_x, conv_w, wout_bf)


def _mixer_short_kernel(bg_ref, cg_ref, xin_ref, mg_ref, yb_ref, x_ref, gt_ref, prev_ref, cw_ref, wout_ref,
                        o_ref, st_ref):
    u = cg_ref[...] * xin_ref[...]
    t_len = u.shape[0]
    ext = [prev_ref[0], prev_ref[1]] + [u[t] for t in range(t_len)]
    cw = cw_ref[...]
    conv = jnp.stack([cw[0:1] * ext[t] + cw[1:2] * ext[t + 1] + cw[2:3] * ext[t + 2] for t in range(t_len)])
    o_ref[...] = _merge_project(bg_ref[...], conv, mg_ref[...], yb_ref[...], x_ref[...], gt_ref[...], wout_ref)
    st_ref[0] = ext[t_len]
    st_ref[1] = ext[t_len + 1]


def _mixer_short(z3, yb3, x3, gt, prev, conv_w, wout_bf):
    t_len, ns, c = x3.shape
    zspec = lambda col: pl.BlockSpec((t_len, ns, c), lambda i, col=col: (0, 0, col))
    full3 = pl.BlockSpec((t_len, ns, c), lambda i: (0, 0, 0))
    return pl.pallas_call(
        _mixer_short_kernel,
        grid=(1,),
        in_specs=[zspec(COL_BG // c), zspec(COL_CG // c), zspec(COL_XIN // c),
                  pl.BlockSpec((t_len, ns, 2 * c), lambda i: (0, 0, COL_MG // (2 * c))),
                  full3, full3,
                  pl.BlockSpec((1, ns, c), lambda i: (0, 0, 0)),
                  pl.BlockSpec((2, ns, c), lambda i: (0, 0, 0)),
                  pl.BlockSpec(conv_w.shape, lambda i: (0, 0)),
                  pl.BlockSpec(wout_bf.shape, lambda i: (0, 0))],
        out_specs=[full3, pl.BlockSpec((2, ns, c), lambda i: (0, 0, 0))],
        out_shape=[jax.ShapeDtypeStruct((t_len, ns, c), F32), jax.ShapeDtypeStruct((2, ns, c), F32)],
        compiler_params=_cparams(("arbitrary",)),
        name="mixer_short",
    )(z3, z3, z3, z3, yb3, x3, gt, prev, conv_w, wout_bf)


def _ffn_finish(act, x, gt, wd_ref, gf_ref):
    shp = act.shape
    proj = _dot(act.reshape(-1, shp[-1]).astype(BF16), wd_ref[...]).reshape(shp[:-1] + (wd_ref.shape[1],))
    x2 = x + gt * proj
    ms = jnp.mean(x2 * x2, axis=-1, keepdims=True)
    return x2 * lax.rsqrt(ms + EPS) * gf_ref[...]


def _ffn_long_kernel(u_ref, v_ref, x_ref, gt_ref, hu_ref, cw_ref, wd_ref, gf_ref, o_ref):
    u = u_ref[...]
    up = hu_ref[0, 0]
    row = lax.broadcasted_iota(jnp.int32, u.shape, 0)
    u1 = jnp.where(row == 0, up[1:2], pltpu.roll(u, 1, 0))
    u2 = jnp.where(row == 0, up[0:1], jnp.where(row == 1, up[1:2], pltpu.roll(u, 2, 0)))
    cw = cw_ref[...]
    uc = cw[0:1] * u2 + cw[1:2] * u1 + cw[2:3] * u
    act = uc * _sigmoid(uc) * v_ref[...]
    o_ref[0] = _ffn_finish(act, x_ref[0], gt_ref[0], wd_ref, gf_ref)


def _ffn_long(uv, x1, gt, halo_u, conv_w, wd_bf, gf_row, *, tm):
    bsz, seq, d = x1.shape
    f = conv_w.shape[1]
    nt = seq // tm
    row = lambda b, i: b * nt + i
    return pl.pallas_call(
        _ffn_long_kernel,
        grid=(bsz, nt),
        in_specs=[pl.BlockSpec((tm, f), lambda b, i: (row(b, i), 0)),
                  pl.BlockSpec((tm, f), lambda b, i: (row(b, i), 1)),
                  pl.BlockSpec((1, tm, d), lambda b, i: (b, i, 0)),
                  pl.BlockSpec((1, 1, d), lambda b, i: (b, 0, 0)),
                  pl.BlockSpec((1, 1, 2, f), lambda b, i: (b, i, 0, 0)),
                  pl.BlockSpec(conv_w.shape, lambda b, i: (0, 0)),
                  pl.BlockSpec(wd_bf.shape, lambda b, i: (0, 0)),
                  pl.BlockSpec(gf_row.shape, lambda b, i: (0, 0))],
        out_specs=pl.BlockSpec((1, tm, d), lambda b, i: (b, i, 0)),
        out_shape=jax.ShapeDtypeStruct((bsz, seq, d), F32),
        compiler_params=_cparams(("parallel", "parallel")),
        name="ffn_long",
    )(uv, uv, x1, gt, halo_u, conv_w, wd_bf, gf_row)


def _ffn_short_kernel(u_ref, v_ref, x_ref, gt_ref, prev_ref, cw_ref, wd_ref, gf_ref, o_ref):
    u = u_ref[...]
    t_len = u.shape[0]
    ext = [prev_ref[0], prev_ref[1]] + [u[t] for t in range(t_len)]
    cw = cw_ref[...]
    uc = jnp.stack([cw[0:1] * ext[t] + cw[1:2] * ext[t + 1] + cw[2:3] * ext[t + 2] for t in range(t_len)])
    act = uc * _sigmoid(uc) * v_ref[...]
    o_ref[...] = _ffn_finish(act, x_ref[...], gt_ref[...], wd_ref, gf_ref)


def _ffn_short(uv3, x3, gt, prev, conv_w, wd_bf, gf_row):
    t_len, ns, d = x3.shape
    f = conv_w.shape[1]
    return pl.pallas_call(
        _ffn_short_kernel,
        grid=(1,),
        in_specs=[pl.BlockSpec((t_len, ns, f), lambda i: (0, 0, 0)),
                  pl.BlockSpec((t_len, ns, f), lambda i: (0, 0, 1)),
                  pl.BlockSpec((t_len, ns, d), lambda i: (0, 0, 0)),
                  pl.BlockSpec((1, ns, d), lambda i: (0, 0, 0)),
                  pl.BlockSpec((2, ns, f), lambda i: (0, 0, 0)),
                  pl.BlockSpec(conv_w.shape, lambda i: (0, 0)),
                  pl.BlockSpec(wd_bf.shape, lambda i: (0, 0)),
                  pl.BlockSpec(gf_row.shape, lambda i: (0, 0))],
        out_specs=pl.BlockSpec((t_len, ns, d), lambda i: (0, 0, 0)),
        out_shape=jax.ShapeDtypeStruct((t_len, ns, d), F32),
        compiler_params=_cparams(("arbitrary",)),
        name="ffn_short",
    )(uv3, uv3, x3, gt, prev, conv_w, wd_bf, gf_row)


def _pad_axis(a, axis, size):
    pad = [(0, 0)] * a.ndim
    pad[axis] = (0, size - a.shape[axis])
    return jnp.pad(a, pad)


def _split_weights(w_in):
    w_main = jnp.concatenate([w_in[:, 0:IN_OFF_KV],
                              w_in[:, IN_OFF_MG:IN_OFF_MG + 2 * D_MODEL],
                              w_in[:, IN_OFF_KV:IN_OFF_AG]], axis=1)
    pieces = []
    for g in range(N_KV):
        for c in range(3):
            off = IN_OFF_AG + c * N_KV * GROUP + g * GROUP
            pieces.append(w_in[:, off:off + GROUP])
        pieces.append(jnp.zeros((w_in.shape[0], LANES - 3 * GROUP), w_in.dtype))
    w_ag = jnp.concatenate(pieces, axis=1)
    return w_main.astype(BF16), w_ag.astype(BF16)


def kernel(x_prompt, x_sample, cache_cmp_k, cache_cmp_v, cache_slc_k, cache_slc_v, cache_win_k, cache_win_v,
           state_conv, state_ffn, page_table, c_prompt, c_sample, w_ada, b_ada, norm1_g, norm2_g, w_in, conv_w,
           cmp_pos_k, cmp_w1_k, cmp_b1_k, cmp_w2_k, cmp_pos_v, cmp_w1_v, cmp_b1_v, cmp_w2_v,
           w_out, w_up, ffn_conv_w, w_down, norm_f_g):
    depth = w_ada.shape[0]
    assert depth == 1, "single trunk layer"
    bsz, seq, d = x_prompt.shape
    ns, n_q, _ = x_sample.shape
    npages = page_table.shape[1]
    past = npages * PAGE
    wb = cache_win_k.shape[2]
    assert d == D_MODEL and seq % KEY_TILE == 0 and seq >= WIN_KEYS and n_q <= SUBLANES and past % SLC_BLK == 0

    w_main, w_ag = _split_weights(w_in[0])
    w_ada_bf = w_ada[0].astype(BF16)
    w_out_bf = w_out[0].astype(BF16)
    w_up_bf = w_up[0].astype(BF16)
    w_down_bf = w_down[0].astype(BF16)
    g1 = norm1_g[0].reshape(1, d)
    g2 = norm2_g[0].reshape(1, d)
    gf = norm_f_g.reshape(1, d)
    cw = conv_w[0]
    fcw = ffn_conv_w[0]
    wts_k = _compress_weights(cmp_pos_k[0], cmp_w1_k[0], cmp_b1_k[0], cmp_w2_k[0])
    wts_v = _compress_weights(cmp_pos_v[0], cmp_w1_v[0], cmp_b1_v[0], cmp_w2_v[0])

    n_c = bsz + ns
    n_c_pad = -(-n_c // SUBLANES) * SUBLANES
    c_all = _pad_axis(jnp.concatenate([c_prompt, c_sample], axis=0), 0, n_c_pad)
    mod = _adaln(c_all, w_ada_bf, b_ada[0].reshape(1, -1))
    mod_p = mod[:bsz].reshape(bsz, 6, 1, d)
    mod_s = mod[bsz:n_c].reshape(ns, 6, d)
    sh1_p, sc1_p, gt1_p, sh2_p, sc2_p, gt2_p = (mod_p[:, k] for k in range(6))
    sh1_s, sc1_s, gt1_s, sh2_s, sc2_s, gt2_s = (mod_s[:, k] for k in range(6))

    def kv_slices(z2d):
        return [z2d[:, COL_KV + k * KV_W:COL_KV + (k + 1) * KV_W] for k in range(6)]

    rows_p = bsz * seq
    tm_p = min(1024, seq)
    z_p, zag_p = _norm_mod_matmul(x_prompt.reshape(rows_p, d), g1, sc1_p, sh1_p, w_main, w_ag,
                                  tm=tm_p, tn=1280, rows_per_mod=seq, name="in_proj_prompt")
    kv_p = [a.reshape(bsz, seq, N_KV, HEAD_DIM) for a in kv_slices(z_p)]
    tab_p = jnp.arange(bsz * (seq // PAGE), dtype=jnp.int32)
    as_pages = lambda a: a.reshape(-1, PAGE, N_KV, HEAD_DIM).transpose(0, 2, 3, 1)[None]
    kc_p = _compress_paged(as_pages(kv_p[0]), tab_p, bsz, seq // PAGE, wts_k)
    vc_p = _compress_paged(as_pages(kv_p[1]), tab_p, bsz, seq // PAGE, wts_v)
    key_major = lambda a: a.transpose(0, 2, 1, 3).astype(BF16)
    feat_major = lambda a: a.transpose(0, 2, 3, 1).astype(BF16)
    qh = z_p[:, COL_Q:COL_Q + d].reshape(bsz, seq, N_KV, GROUP, HEAD_DIM).transpose(0, 2, 3, 1, 4).astype(BF16)
    n_slc_p = seq // SLC_BLK
    n_lanes_p = -(-n_slc_p // LANES) * LANES
    nch_p = seq // CMP_STRIDE
    ovl_p = _overlap(nch_p, nch_p - 1, n_slc_p, n_lanes_p).T
    yb_h = _prompt_attention(qh, kc_p, vc_p.transpose(0, 1, 3, 2), key_major(kv_p[2]), feat_major(kv_p[3]),
                             key_major(kv_p[4]), feat_major(kv_p[5]), zag_p.reshape(bsz, seq, AG_W), ovl_p, seq)
    yb_p = yb_h.transpose(0, 4, 1, 2, 3).reshape(bsz, seq, d)

    tm_m = min(512, seq)
    nt_m = seq // tm_m
    z4 = z_p.reshape(bsz, nt_m, tm_m, Z_W)
    conv0 = jnp.zeros((bsz, 1, 2, d), F32)
    halo_c = jnp.concatenate([conv0, z4[:, :-1, tm_m - 2:, COL_CG:COL_CG + d]], axis=1)
    halo_x = jnp.concatenate([jnp.ones((bsz, 1, 2, d), F32), z4[:, :-1, tm_m - 2:, COL_XIN:COL_XIN + d]], axis=1)
    x1_p, tail_p = _mixer_long(z_p, yb_p, x_prompt, gt1_p, halo_c, halo_x, cw, w_out_bf, tm=tm_m)
    conv_p = tail_p[:, -1]

    uv_p = _norm_mod_matmul(x1_p.reshape(rows_p, d), g2, sc2_p, sh2_p, w_up_bf, None,
                            tm=tm_p, tn=1408, rows_per_mod=seq, name="up_proj_prompt")
    tm_f = min(256, seq)
    nt_f = seq // tm_f
    u4 = uv_p.reshape(bsz, nt_f, tm_f, 2 * D_FF)
    halo_u = jnp.concatenate([jnp.zeros((bsz, 1, 2, D_FF), F32), u4[:, :-1, tm_f - 2:, :D_FF]], axis=1)
    y_prompt = _ffn_long(uv_p, x1_p, gt2_p, halo_u, fcw, w_down_bf, gf, tm=tm_f)
    ffn_p = u4[:, -1, tm_f - 2:, :D_FF]

    wbp = min(WINDOW, seq)
    outs_p = [a[None] for a in kv_p[:4]] + [kv_p[4][None, :, seq - wbp:], kv_p[5][None, :, seq - wbp:],
                                            conv_p[None], ffn_p[None]]

    rows_s = n_q * ns
    x_s3 = x_sample.transpose(1, 0, 2)
    tile_t = lambda a: jnp.tile(a, (n_q, 1))[None]
    z_s, zag_s = _norm_mod_matmul(x_s3.reshape(rows_s, d), g1, tile_t(sc1_s), tile_t(sh1_s), w_main, w_ag,
                                  tm=rows_s, tn=1280, rows_per_mod=rows_s, name="in_proj_sample")
    kv_s = [a.reshape(n_q, ns, N_KV, HEAD_DIM).transpose(1, 0, 2, 3) for a in kv_slices(z_s)]
    pool = lambda c: c.transpose(0, 1, 3, 4, 2)
    tab_s = page_table.reshape(-1).astype(jnp.int32)
    kc_s = _compress_paged(pool(cache_cmp_k), tab_s, ns, npages, wts_k)
    vc_s = _compress_paged(pool(cache_cmp_v), tab_s, ns, npages, wts_v)

    q_s = z_s[:, COL_Q:COL_Q + d].reshape(n_q, ns, N_KV, GROUP, HEAD_DIM)
    q1 = _pad_axis(q_s.transpose(1, 2, 3, 0, 4), 3, SUBLANES).reshape(ns, N_KV, GROUP * SUBLANES, HEAD_DIM)
    q2 = _pad_axis(q_s.transpose(1, 2, 0, 3, 4), 3, SUBLANES)
    n_slc_s = -(-(past + n_q) // SLC_BLK)
    n_lanes_s = -(-n_slc_s // LANES) * LANES
    nch_s = past // CMP_STRIDE
    ovl_s = _overlap(nch_s, nch_s - 1, n_slc_s, n_lanes_s)
    oc1, idx = _sample_cmp(q1.astype(BF16), kc_s, vc_s, ovl_s, past, n_slc_s)
    oc2 = _pad_axis(oc1.reshape(ns, N_KV, GROUP, SUBLANES, HEAD_DIM)[:, :, :, :n_q].transpose(0, 1, 3, 2, 4),
                    3, SUBLANES)

    logical = jnp.minimum(idx[:, :, :n_q, :TOPK], past // SLC_BLK - 1).reshape(ns, -1) // (PAGE // SLC_BLK)
    hit = logical[:, :, None] == jnp.arange(npages, dtype=jnp.int32)[None, None, :]
    sel_pages = jnp.sum(jnp.where(hit, page_table[:, None, :], 0), axis=-1).reshape(-1).astype(jnp.int32)

    new_rows = [_pad_axis(a.transpose(0, 2, 1, 3), 2, SUBLANES) for a in kv_s[2:]]
    ag_s = zag_s.reshape(n_q, ns, N_KV, LANES)[..., :3 * GROUP].reshape(n_q, ns, N_KV, 3, GROUP)
    ag2 = _pad_axis(_pad_axis(ag_s.transpose(1, 2, 0, 4, 3), 3, SUBLANES), 4, LANES)
    ag2 = ag2.reshape(ns, N_KV, n_q * SUBLANES, LANES)
    y_h = _sample_selwin(sel_pages, pool(cache_slc_k), pool(cache_slc_v), pool(cache_win_k), pool(cache_win_v),
                         idx, q2.astype(BF16), new_rows[0], new_rows[1], new_rows[2], new_rows[3], oc2, ag2, past)
    yb_s3 = y_h[:, :, :, :GROUP].transpose(2, 0, 1, 3, 4).reshape(n_q, ns, d)

    new_tail = lambda a: jnp.pad(a.transpose(0, 2, 3, 1), ((0, 0), (0, 0), (0, 0), (LANES - n_q, 0)))
    win_k_s = _win_shift(pool(cache_win_k), new_tail(kv_s[4]), n_q).transpose(0, 1, 4, 2, 3)
    win_v_s = _win_shift(pool(cache_win_v), new_tail(kv_s[5]), n_q).transpose(0, 1, 4, 2, 3)

    z_s3 = z_s.reshape(n_q, ns, Z_W)
    x1_s3, conv_s3 = _mixer_short(z_s3, yb_s3, x_s3, gt1_s[None], state_conv[0].transpose(1, 0, 2), cw, w_out_bf)
    uv_s = _norm_mod_matmul(x1_s3.reshape(rows_s, d), g2, tile_t(sc2_s), tile_t(sh2_s), w_up_bf, None,
                            tm=rows_s, tn=1408, rows_per_mod=rows_s, name="up_proj_sample")
    uv_s3 = uv_s.reshape(n_q, ns, 2 * D_FF)
    y_s3 = _ffn_short(uv_s3, x1_s3, gt2_s[None], state_ffn[0].transpose(1, 0, 2), fcw, w_down_bf, gf)
    y_sample = y_s3.transpose(1, 0, 2)
    ffn_ext = jnp.concatenate([state_ffn[0], uv_s3[:, :, :D_FF].transpose(1, 0, 2)], axis=1)
    ffn_s = ffn_ext[:, n_q:]
    conv_s = conv_s3.transpose(1, 0, 2)

    outs_s = [a[None] for a in kv_s[:4]] + [win_k_s, win_v_s, conv_s[None], ffn_s[None]]
    return (y_prompt, y_sample, *outs_p, *outs_s)
```
